```python
import jax, jax.numpy as jnp
from jax import lax
import numpy as np

D_MODEL = 1024
BATCH = 32
SEQ = 2048
DEPTH = 1

SB_HEAD_DIM = 64
SB_WIDTH = D_MODEL // 2
SB_HEADS = SB_WIDTH // SB_HEAD_DIM
GLA_HEADS = 4
GLA_WIDTH = D_MODEL - SB_WIDTH
GLA_DV = GLA_WIDTH // GLA_HEADS
GLA_DK = GLA_DV // 2
GLA_KEY_WIDTH = GLA_HEADS * GLA_DK
GLA_GATE_RANK = 16
GLA_GATE_NORMALIZER = 16.0
GLA_CHUNK = 64
Q_BLOCK = 128
D_FF = 2816
CONV_WIDTH = 3
EPS = 1e-6

IN_SPLITS = (SB_WIDTH, SB_WIDTH, SB_WIDTH,
             GLA_KEY_WIDTH, GLA_KEY_WIDTH, GLA_WIDTH,
             GLA_GATE_RANK, GLA_WIDTH)
IN_COLS = sum(IN_SPLITS)

kernel_name = "hybrid_stickbreak_gla_convffn"


def rms_norm(x, g):
    xf = x.astype(jnp.float32)
    y = xf * lax.rsqrt(jnp.mean(xf * xf, axis=-1, keepdims=True) + EPS)
    return (y * g.astype(jnp.float32)).astype(x.dtype)


def head_rms_norm(o, g):
    B, H, S, d = o.shape
    of = o.astype(jnp.float32)
    of = of * lax.rsqrt(jnp.mean(of * of, axis=-1, keepdims=True) + EPS)
    of = jnp.transpose(of, (0, 2, 1, 3)).reshape(B, S, H * d)
    return (of * g.astype(jnp.float32)).astype(o.dtype)


def to_heads(t, n_heads):
    B, S, W = t.shape
    return jnp.transpose(t.reshape(B, S, n_heads, W // n_heads), (0, 2, 1, 3))


def stick_breaking_attention(q, k, v):
    B, H, S, d = q.shape
    scale = d ** -0.5
    outs = []
    for i in range(S // Q_BLOCK):
        q0 = i * Q_BLOCK
        kn = q0 + Q_BLOCK
        z = jnp.einsum('bhqd,bhkd->bhqk', q[:, :, q0:kn], k[:, :, :kn]).astype(jnp.float32) * scale
        t_idx = q0 + jnp.arange(Q_BLOCK)[:, None]
        s_idx = jnp.arange(kn)[None, :]
        strict = s_idx < t_idx
        log1m = jnp.where(strict, -jax.nn.softplus(z), 0.0)
        after = lax.cumsum(log1m, axis=3, reverse=True) - log1m
        w = jnp.where(strict, jnp.exp(jax.nn.log_sigmoid(z) + after), 0.0)
        outs.append(jnp.einsum('bhqk,bhkd->bhqd', w.astype(v.dtype), v[:, :, :kn]))
    return jnp.concatenate(outs, axis=2)


def gla_chunked(q, k, v, log_a):
    B, H, S, dk = q.shape
    dv = v.shape[-1]
    C = GLA_CHUNK
    N = S // C
    f32 = jnp.float32
    qf = q.astype(f32).reshape(B, H, N, C, dk) * (dk ** -0.5)
    kf = k.astype(f32).reshape(B, H, N, C, dk)
    vf = v.astype(f32).reshape(B, H, N, C, dv)
    b = jnp.cumsum(log_a.astype(f32).reshape(B, H, N, C, dk), axis=3)
    b_last = b[:, :, :, -1:, :]
    q_dec = qf * jnp.exp(b)
    k_inv = kf * jnp.exp(-b)
    k_end = kf * jnp.exp(b_last - b)
    causal = jnp.tril(jnp.ones((C, C), dtype=f32))
    attn = jnp.einsum('bhnik,bhnjk->bhnij', q_dec, k_inv) * causal
    o_intra = jnp.einsum('bhnij,bhnjv->bhniv', attn, vf)
    chunk_kv = jnp.einsum('bhnck,bhncv->bhnkv', k_end, vf)
    decay = jnp.exp(b_last[:, :, :, 0, :])

    def step(state, inp):
        kv_n, dec_n = inp
        return dec_n[..., None] * state + kv_n, state

    init = jnp.zeros((B, H, dk, dv), dtype=f32)
    _, prev = lax.scan(step, init, (jnp.moveaxis(chunk_kv, 2, 0), jnp.moveaxis(decay, 2, 0)))
    prev = jnp.moveaxis(prev, 0, 2)
    o_inter = jnp.einsum('bhnck,bhnkv->bhncv', q_dec, prev)
    return (o_intra + o_inter).reshape(B, H, S, dv).astype(v.dtype)


def causal_depthwise_conv(u, w, bias):
    S = u.shape[1]
    up = jnp.pad(u, ((0, 0), (CONV_WIDTH - 1, 0), (0, 0)))
    out = bias
    for tap in range(CONV_WIDTH):
        out = out + w[tap] * up[:, tap:tap + S]
    return out


def setup_inputs(seed: int = 0) -> dict:
    key = jax.random.key(seed)
    ks = jax.random.split(key, 16)
    nrm = lambda k, shape, s: jax.random.normal(k, shape, dtype=jnp.float32) * s
    gain = lambda k, shape: 1.0 + 0.02 * jax.random.normal(k, shape, dtype=jnp.float32)
    return {
        "x": nrm(ks[0], (BATCH, SEQ, D_MODEL), 1.0),
        "attn_norm_g": gain(ks[1], (DEPTH, D_MODEL)),
        "w_in": nrm(ks[2], (DEPTH, D_MODEL, IN_COLS), D_MODEL ** -0.5),
        "w_gate_up": nrm(ks[3], (DEPTH, GLA_GATE_RANK, GLA_KEY_WIDTH), GLA_GATE_RANK ** -0.5),
        "b_gate_up": nrm(ks[4], (DEPTH, GLA_KEY_WIDTH), 0.1),
        "sb_out_g": gain(ks[5], (DEPTH, SB_WIDTH)),
        "gla_out_g": gain(ks[6], (DEPTH, GLA_WIDTH)),
        "w_out": nrm(ks[7], (DEPTH, D_MODEL, D_MODEL), D_MODEL ** -0.5),
        "ffn_norm_g": gain(ks[8], (DEPTH, D_MODEL)),
        "w_ffn_up": nrm(ks[9], (DEPTH, D_MODEL, 2 * D_FF), D_MODEL ** -0.5),
        "conv_w": nrm(ks[10], (DEPTH, CONV_WIDTH, 2 * D_FF), CONV_WIDTH ** -0.5),
        "conv_b": nrm(ks[11], (DEPTH, 2 * D_FF), 0.01),
        "w_ffn_down": nrm(ks[12], (DEPTH, D_FF, D_MODEL), D_FF ** -0.5),
        "final_norm_g": gain(ks[13], (D_MODEL,)),
    }


def reference(x, attn_norm_g, w_in, w_gate_up, b_gate_up, sb_out_g, gla_out_g, w_out,
              ffn_norm_g, w_ffn_up, conv_w, conv_b, w_ffn_down, final_norm_g):
    offsets = list(np.cumsum(IN_SPLITS)[:-1])
    for l in range(DEPTH):
        h = rms_norm(x, attn_norm_g[l])
        proj = h @ w_in[l]
        sb_q, sb_k, sb_v, g_q, g_k, g_v, g_lr, g_og = jnp.split(proj, offsets, axis=-1)

        o_sb = stick_breaking_attention(to_heads(sb_q, SB_HEADS), to_heads(sb_k, SB_HEADS),
                                        to_heads(sb_v, SB_HEADS))
        o_sb = head_rms_norm(o_sb, sb_out_g[l])

        log_a = jax.nn.log_sigmoid((g_lr @ w_gate_up[l] + b_gate_up[l]).astype(jnp.float32)) / GLA_GATE_NORMALIZER
        o_gla = gla_chunked(to_heads(g_q, GLA_HEADS), to_heads(g_k, GLA_HEADS),
                            to_heads(g_v, GLA_HEADS), to_heads(log_a, GLA_HEADS))
        o_gla = head_rms_norm(o_gla, gla_out_g[l]) * jax.nn.silu(g_og)

        x = x + jnp.concatenate([o_sb, o_gla], axis=-1) @ w_out[l]

        h = rms_norm(x, ffn_norm_g[l])
        u = causal_depthwise_conv(h @ w_ffn_up[l], conv_w[l], conv_b[l])
        a, val = jnp.split(u, 2, axis=-1)
        x = x + (jax.nn.silu(a) * val) @ w_ffn_down[l]
    return rms_norm(x, final_norm_g)
```

```python
import functools

import jax
import jax.numpy as jnp
from jax import lax
from jax.experimental import pallas as pl
from jax.experimental.pallas import tpu as pltpu

F32 = jnp.float32
BF16 = jnp.bfloat16

D_MODEL = 1024
SB_HEAD_DIM = 64
SB_WIDTH = 512
SB_HEADS = SB_WIDTH // SB_HEAD_DIM
GLA_HEADS = 4
GLA_WIDTH = 512
GLA_DV = GLA_WIDTH // GLA_HEADS
GLA_DK = GLA_DV // 2
GLA_KEY_WIDTH = GLA_HEADS * GLA_DK
GLA_GATE_RANK = 16
GLA_GATE_NORMALIZER = 16.0
GLA_CHUNK = 64
D_FF = 2816
CONV_WIDTH = 3
EPS = 1e-6

LANES = 128
SUBLANES = 8
VMEM_LIMIT_BYTES = 56 * 1024 * 1024

P_SB_Q, P_SB_K, P_SB_V = 0, 512, 1024
P_G_Q, P_G_K, P_G_V, P_G_OG = 1536, 1792, 2048, 2560
P_COLS = 3072

SB_DEAD_LOG_WEIGHT = -88.0


def _dot(a, b):
    return jnp.dot(a, b, preferred_element_type=F32)


def _dot_nt(a, b):
    return lax.dot_general(a, b, (((1,), (1,)), ((), ())), preferred_element_type=F32)


def _dot_tn(a, b):
    return lax.dot_general(a, b, (((0,), (0,)), ((), ())), preferred_element_type=F32)


def _split_bf16(x):
    hi = x.astype(BF16)
    lo = (x - hi.astype(F32)).astype(BF16)
    return hi, lo


def _log_sigmoid(x):
    return jnp.minimum(x, 0.0) - jnp.log(1.0 + jnp.exp(-jnp.abs(x)))


def _inproj_kernel(x_ref, g_ref, w_ref, wlr_ref, wg_ref, bg_ref, p_ref, la_ref, *, col_chunk):
    x = x_ref[...]
    ms = jnp.mean(x * x, axis=-1, keepdims=True)
    y = (x * lax.rsqrt(ms + EPS) * g_ref[...]).astype(BF16)
    for c in range(0, P_COLS, col_chunk):
        p_ref[:, c:c + col_chunk] = _dot(y, w_ref[:, c:c + col_chunk]).astype(BF16)
    g_lr = _dot(y, wlr_ref[...])
    logit = _dot(g_lr.astype(BF16), wg_ref[...]) + bg_ref[...]
    la_ref[...] = _log_sigmoid(logit) * (1.0 / GLA_GATE_NORMALIZER)


def _inproj(x2d, g, w_main, w_lr, w_gate, b_gate, *, tm):
    T = x2d.shape[0]
    const = lambda i: (0, 0)
    return pl.pallas_call(
        functools.partial(_inproj_kernel, col_chunk=512),
        grid=(T // tm,),
        in_specs=[
            pl.BlockSpec((tm, D_MODEL), lambda i: (i, 0)),
            pl.BlockSpec((1, D_MODEL), const),
            pl.BlockSpec((D_MODEL, P_COLS), const),
            pl.BlockSpec((D_MODEL, LANES), const),
            pl.BlockSpec((LANES, GLA_KEY_WIDTH), const),
            pl.BlockSpec((1, GLA_KEY_WIDTH), const),
        ],
        out_specs=[
            pl.BlockSpec((tm, P_COLS), lambda i: (i, 0)),
            pl.BlockSpec((tm, GLA_KEY_WIDTH), lambda i: (i, 0)),
        ],
        out_shape=[
            jax.ShapeDtypeStruct((T, P_COLS), BF16),
            jax.ShapeDtypeStruct((T, GLA_KEY_WIDTH), F32),
        ],
        compiler_params=pltpu.CompilerParams(
            dimension_semantics=("arbitrary",), vmem_limit_bytes=VMEM_LIMIT_BYTES),
        name="inproj",
    )(x2d, g, w_main, w_lr, w_gate, b_gate)


def _sb_kernel(q_ref, k_ref, v_ref, g_ref, o_ref, qm_ref, acc_ref, carry_ref, *, tq, tk):
    n_pairs = SB_HEADS // 2
    i = pl.program_id(1)
    lane = lax.broadcasted_iota(jnp.int32, (1, LANES), 1)
    low_half = lane < SB_HEAD_DIM

    for p in range(n_pairs):
        qp = q_ref[:, p * LANES:(p + 1) * LANES] * (-(SB_HEAD_DIM ** -0.5))
        qm_ref[2 * p] = jnp.where(low_half, qp, 0).astype(BF16)
        qm_ref[2 * p + 1] = jnp.where(low_half, 0, qp).astype(BF16)
    acc_ref[...] = jnp.zeros_like(acc_ref)
    carry_ref[...] = jnp.zeros_like(carry_ref)

    row = lax.broadcasted_iota(jnp.int32, (tk, 2 * tk), 0)
    col = lax.broadcasted_iota(jnp.int32, (tk, 2 * tk), 1)
    later_or_total = jnp.where((row > col) | (col >= tk), 1.0, 0.0).astype(BF16)
    t_idx = lax.broadcasted_iota(jnp.int32, (tk, tk), 0)
    s_idx = lax.broadcasted_iota(jnp.int32, (tk, tk), 1)
    strict = s_idx < t_idx

    def block(r0, nrows, key_start, diagonal, first):
        rows = pl.ds(r0, nrows)
        for p in range(n_pairs):
            kp = k_ref[pl.ds(key_start, tk), p * LANES:(p + 1) * LANES]
            vp = v_ref[pl.ds(key_start, tk), p * LANES:(p + 1) * LANES]
            v_half = (jnp.where(low_half, vp, 0).astype(BF16), jnp.where(low_half, 0, vp).astype(BF16))
            pv = None
            for hh in range(2):
                h = 2 * p + hh
                zn = _dot_nt(qm_ref[h, rows, :], kp)
                tail = jnp.log(1.0 + jnp.exp(-jnp.abs(zn)))
                log1m = jnp.minimum(zn, 0.0) - tail
                logsig = log1m - zn
                if diagonal:
                    log1m = jnp.where(strict, log1m, 0.0)
                hi, lo = _split_bf16(log1m)
                sums = _dot(hi, later_or_total) + _dot(lo, later_or_total)
                after = sums[:, :tk]
                if not first:
                    after = after + carry_ref[h, rows, :]
                w = jnp.exp(logsig + after)
                if diagonal:
                    w = jnp.where(strict, w, 0.0)
                contrib = _dot(w.astype(BF16), v_half[hh])
                pv = contrib if pv is None else pv + contrib
                if first:
                    carry_ref[h, rows, :] = sums[:, tk:]
                else:
                    carry_ref[h, rows, :] += sums[:, tk:]
            if first:
                acc_ref[p, rows, :] = pv
            else:
                acc_ref[p, rows, :] += pv

    n_sub = tq // tk
    for r in range(n_sub - 1, -1, -1):
        block(r * tk, tk, i * tq + r * tk, diagonal=True, first=True)
        for r2 in range(r + 1, n_sub):
            block(r2 * tk, tk, i * tq + r * tk, diagonal=False, first=False)

    def cond(state):
        j, live = state
        return jnp.logical_and(j >= 0, live > SB_DEAD_LOG_WEIGHT)

    def body(state):
        j, _ = state
        block(0, tq, j * tk, diagonal=False, first=False)
        m = carry_ref[0]
        for h in range(1, SB_HEADS):
            m = jnp.maximum(m, carry_ref[h])
        return j - 1, jnp.max(m)

    lax.while_loop(cond, body, (i * n_sub - 1, jnp.float32(0.0)))

    for p in range(n_pairs):
        o = acc_ref[p]
        o2 = o * o
        ss_lo = jnp.sum(jnp.where(low_half, o2, 0.0), axis=-1, keepdims=True)
        ss_hi = jnp.sum(jnp.where(low_half, 0.0, o2), axis=-1, keepdims=True)
        inv = jnp.where(low_half,
                        lax.rsqrt(ss_lo * (1.0 / SB_HEAD_DIM) + EPS),
                        lax.rsqrt(ss_hi * (1.0 / SB_HEAD_DIM) + EPS))
        o_ref[:, p * LANES:(p + 1) * LANES] = (o * inv * g_ref[:, p * LANES:(p + 1) * LANES]).astype(BF16)


def _sb_attention(p_act, g, *, batch, seq, tq, tk):
    T = batch * seq
    nq = seq // tq
    n_pairs = SB_HEADS // 2
    return pl.pallas_call(
        functools.partial(_sb_kernel, tq=tq, tk=tk),
        grid=(batch, nq),
        in_specs=[
            pl.BlockSpec((tq, SB_WIDTH), lambda b, i: (b * nq + i, P_SB_Q // SB_WIDTH)),
            pl.BlockSpec((seq, SB_WIDTH), lambda b, i: (b, P_SB_K // SB_WIDTH)),
            pl.BlockSpec((seq, SB_WIDTH), lambda b, i: (b, P_SB_V // SB_WIDTH)),
            pl.BlockSpec((1, SB_WIDTH), lambda b, i: (0, 0)),
        ],
        out_specs=pl.BlockSpec((tq, SB_WIDTH), lambda b, i: (b * nq + i, 0)),
        out_shape=jax.ShapeDtypeStruct((T, SB_WIDTH), BF16),
        scratch_shapes=[
            pltpu.VMEM((SB_HEADS, tq, LANES), BF16),
            pltpu.VMEM((n_pairs, tq, LANES), F32),
            pltpu.VMEM((SB_HEADS, tq, LANES), F32),
        ],
        compiler_params=pltpu.CompilerParams(
            dimension_semantics=("arbitrary", "arbitrary"), vmem_limit_bytes=VMEM_LIMIT_BYTES),
        name="sb",
    )(p_act, p_act, p_act, g)


def _gla_kernel(q_ref, k_ref, v_ref, og_ref, la_ref, g_ref, o_ref, state_ref, *, seq):
    C = GLA_CHUNK
    n_pairs = GLA_HEADS // 2
    lane = lax.broadcasted_iota(jnp.int32, (1, LANES), 1)
    low_half = lane < GLA_DK
    ti = lax.broadcasted_iota(jnp.int32, (C, C), 0)
    tj = lax.broadcasted_iota(jnp.int32, (C, C), 1)
    causal = tj <= ti
    incl = jnp.where(causal, 1.0, 0.0).astype(BF16)
    state_ref[...] = jnp.zeros_like(state_ref)

    def chunk(c, carry):
        rows = pl.ds(pl.multiple_of(c * C, C), C)
        for p in range(n_pairs):
            ks = slice(p * LANES, (p + 1) * LANES)
            la_hi, la_lo = _split_bf16(la_ref[rows, ks])
            b = _dot(incl, la_hi) + _dot(incl, la_lo)
            b_last = b[C - 1:C, :]
            q = q_ref[rows, ks].astype(F32) * (GLA_DK ** -0.5)
            k = k_ref[rows, ks].astype(F32)
            q_dec = q * jnp.exp(b)
            k_inv = (k * jnp.exp(-b)).astype(BF16)
            k_end = k * jnp.exp(b_last - b)
            decay = jnp.exp(b_last)
            k_end_half = (jnp.where(low_half, k_end, 0.0).astype(BF16),
                          jnp.where(low_half, 0.0, k_end).astype(BF16))
            q_half = (jnp.where(low_half, q_dec, 0.0).astype(BF16),
                      jnp.where(low_half, 0.0, q_dec).astype(BF16))
            state_t = state_ref[p]
            state_b = state_t.astype(BF16)
            kv = None
            for hh in range(2):
                h = 2 * p + hh
                vs = slice(h * GLA_DV, (h + 1) * GLA_DV)
                v = v_ref[rows, vs]
                attn = jnp.where(causal, _dot_nt(q_half[hh], k_inv), 0.0)
                o = _dot(attn.astype(BF16), v) + _dot_nt(q_half[hh], state_b)
                inv = lax.rsqrt(jnp.mean(o * o, axis=-1, keepdims=True) + EPS)
                gate = og_ref[rows, vs].astype(F32)
                gate = gate / (1.0 + jnp.exp(-gate))
                o_ref[rows, vs] = (o * inv * g_ref[:, vs] * gate).astype(BF16)
                upd = _dot_tn(v, k_end_half[hh])
                kv = upd if kv is None else kv + upd
            state_ref[p] = state_t * decay + kv
        return carry

    lax.fori_loop(0, seq // C, chunk, 0)


def _gla(p_act, log_a, g, *, batch, seq):
    T = batch * seq
    n_pairs = GLA_HEADS // 2
    return pl.pallas_call(
        functools.partial(_gla_kernel, seq=seq),
        grid=(batch,),
        in_specs=[
            pl.BlockSpec((seq, GLA_KEY_WIDTH), lambda b: (b, P_G_Q // GLA_KEY_WIDTH)),
            pl.BlockSpec((seq, GLA_KEY_WIDTH), lambda b: (b, P_G_K // GLA_KEY_WIDTH)),
            pl.BlockSpec((seq, GLA_WIDTH), lambda b: (b, P_G_V // GLA_WIDTH)),
            pl.BlockSpec((seq, GLA_WIDTH), lambda b: (b, P_G_OG // GLA_WIDTH)),
            pl.BlockSpec((seq, GLA_KEY_WIDTH), lambda b: (b, 0)),
            pl.BlockSpec((1, GLA_WIDTH), lambda b: (0, 0)),
        ],
        out_specs=pl.BlockSpec((seq, GLA_WIDTH), lambda b: (b, 0)),
        out_shape=jax.ShapeDtypeStruct((T, GLA_WIDTH), BF16),
        scratch_shapes=[pltpu.VMEM((n_pairs, GLA_DV, LANES), F32)],
        compiler_params=pltpu.CompilerParams(
            dimension_semantics=("arbitrary",), vmem_limit_bytes=VMEM_LIMIT_BYTES),
        name="gla",
    )(p_act, p_act, p_act, p_act, log_a, g)


def _rms(x, g):
    ms = jnp.mean(x * x, axis=-1, keepdims=True)
    return x * lax.rsqrt(ms + EPS) * g


def _ffn_kernel(x_ref, osb_ref, ogla_ref, wo_sb_ref, wo_gla_ref, g2_ref, wup_ref, cw_ref, cb_ref,
                wdn_ref, gf_ref, out_ref, x1_ref, h2_ref, acc_ref, ua_ref, uv_ref, halo_ref,
                *, tm, fc, tiles_per_seq):
    t = pl.program_id(0)
    n_chunks = D_FF // fc
    H = SUBLANES

    @pl.when(t % tiles_per_seq == 0)
    def _():
        halo_ref[...] = jnp.zeros_like(halo_ref)

    x1 = x_ref[...] + _dot(osb_ref[...], wo_sb_ref[...]) + _dot(ogla_ref[...], wo_gla_ref[...])
    x1_ref[...] = x1
    h2_ref[...] = _rms(x1, g2_ref[...]).astype(BF16)

    for c in range(n_chunks):
        conv = []
        for part, buf in ((0, ua_ref), (1, uv_ref)):
            cols = slice(part * D_FF + c * fc, part * D_FF + (c + 1) * fc)
            u = _dot(h2_ref[...], wup_ref[:, cols])
            buf[0:H, :] = halo_ref[part, c]
            buf[H:H + tm, :] = u
            halo_ref[part, c] = u[tm - H:tm, :]
            cw = cw_ref[:, cols]
            conv.append(cb_ref[:, cols] + cw[2:3, :] * u
                        + cw[1:2, :] * buf[H - 1:H - 1 + tm, :]
                        + cw[0:1, :] * buf[H - 2:H - 2 + tm, :])
        a, val = conv
        gated = (a / (1.0 + jnp.exp(-a)) * val).astype(BF16)
        down = _dot(gated, wdn_ref[c * fc:(c + 1) * fc, :])
        if c == 0:
            acc_ref[...] = down
        else:
            acc_ref[...] += down

    out_ref[...] = _rms(x1_ref[...] + acc_ref[...], gf_ref[...])


def _ffn(x2d, o_sb, o_gla, wo_sb, wo_gla, g2, w_up, conv_w, conv_b, w_down, gf, *, tm, fc, seq):
    T = x2d.shape[0]
    n_chunks = D_FF // fc
    const = lambda i: (0, 0)
    resident = functools.partial(pl.BlockSpec, index_map=const, pipeline_mode=pl.Buffered(1))
    return pl.pallas_call(
        functools.partial(_ffn_kernel, tm=tm, fc=fc, tiles_per_seq=seq // tm),
        grid=(T // tm,),
        in_specs=[
            pl.BlockSpec((tm, D_MODEL), lambda i: (i, 0)),
            pl.BlockSpec((tm, SB_WIDTH), lambda i: (i, 0)),
            pl.BlockSpec((tm, GLA_WIDTH), lambda i: (i, 0)),
            resident((SB_WIDTH, D_MODEL)),
            resident((GLA_WIDTH, D_MODEL)),
            resident((1, D_MODEL)),
            resident((D_MODEL, 2 * D_FF)),
            resident((CONV_WIDTH, 2 * D_FF)),
            resident((1, 2 * D_FF)),
            resident((D_FF, D_MODEL)),
            resident((1, D_MODEL)),
        ],
        out_specs=pl.BlockSpec((tm, D_MODEL), lambda i: (i, 0)),
        out_shape=jax.ShapeDtypeStruct((T, D_MODEL), F32),
        scratch_shapes=[
            pltpu.VMEM((tm, D_MODEL), F32),
            pltpu.VMEM((tm, D_MODEL), BF16),
            pltpu.VMEM((tm, D_MODEL), F32),
            pltpu.VMEM((SUBLANES + tm, fc), F32),
            pltpu.VMEM((SUBLANES + tm, fc), F32),
            pltpu.VMEM((2, n_chunks, SUBLANES, fc), F32),
        ],
        compiler_params=pltpu.CompilerParams(
            dimension_semantics=("arbitrary",), vmem_limit_bytes=VMEM_LIMIT_BYTES),
        name="ffn",
    )(x2d, o_sb, o_gla, wo_sb, wo_gla, g2, w_up, conv_w, conv_b, w_down, gf)


def kernel(x, attn_norm_g, w_in, w_gate_up, b_gate_up, sb_out_g, gla_out_g, w_out, ffn_norm_g,
           w_ffn_up, conv_w, conv_b, w_ffn_down, final_norm_g):
    batch, seq, d_model = x.shape
    assert d_model == D_MODEL and attn_norm_g.shape[0] == 1
    T = batch * seq
    x2d = x.reshape(T, D_MODEL)

    lr0 = P_G_OG
    w_in0 = w_in[0]
    w_main = jnp.concatenate([w_in0[:, :lr0], w_in0[:, lr0 + GLA_GATE_RANK:]], axis=1).astype(BF16)
    w_lr = jnp.pad(w_in0[:, lr0:lr0 + GLA_GATE_RANK], ((0, 0), (0, LANES - GLA_GATE_RANK))).astype(BF16)
    w_gate = jnp.pad(w_gate_up[0], ((0, LANES - GLA_GATE_RANK), (0, 0))).astype(BF16)

    tm = min(512, seq)
    p_act, log_a = _inproj(x2d, attn_norm_g, w_main, w_lr, w_gate, b_gate_up, tm=tm)
    o_sb = _sb_attention(p_act, sb_out_g, batch=batch, seq=seq, tq=min(128, seq), tk=128)
    o_gla = _gla(p_act, log_a, gla_out_g, batch=batch, seq=seq)
    out = _ffn(x2d, o_sb, o_gla,
               w_out[0, :SB_WIDTH].astype(BF16), w_out[0, SB_WIDTH:].astype(BF16),
               ffn_norm_g, w_ffn_up[0].astype(BF16), conv_w[0], conv_b, w_ffn_down[0].astype(BF16),
               final_norm_g.reshape(1, D_MODEL), tm=tm, fc=256, seq=seq)
    return out.reshape(batch, seq, D_MODEL)
```

```python
import functools

import jax
import jax.numpy as jnp
from jax import lax
from jax.experimental import pallas as pl
from jax.experimental.pallas import tpu as pltpu

F32 = jnp.float32
BF16 = jnp.bfloat16

D_MODEL = 1024
SB_HEAD_DIM = 64
SB_WIDTH = 512
SB_HEADS = SB_WIDTH // SB_HEAD_DIM
GLA_HEADS = 4
GLA_WIDTH = 512
GLA_DV = GLA_WIDTH // GLA_HEADS
GLA_DK = GLA_DV // 2
GLA_KEY_WIDTH = GLA_HEADS * GLA_DK
GLA_GATE_RANK = 16
GLA_GATE_NORMALIZER = 16.0
GLA_CHUNK = 64
D_FF = 2816
CONV_WIDTH = 3
EPS = 1e-6

LANES = 128
SUBLANES = 8
VMEM_LIMIT_BYTES = 56 * 1024 * 1024

P_SB_Q, P_SB_K, P_SB_V = 0, 512, 1024
P_G_Q, P_G_K, P_G_V, P_G_OG = 1536, 1792, 2048, 2560
P_COLS = 3072

LOG2_E = 1.4426950408889634
SB_DEAD_LOG2_WEIGHT = -127.0


def _dot(a, b):
    return jnp.dot(a, b, preferred_element_type=F32)


def _dot_nt(a, b):
    return lax.dot_general(a, b, (((1,), (1,)), ((), ())), preferred_element_type=F32)


def _dot_tn(a, b):
    return lax.dot_general(a, b, (((0,), (0,)), ((), ())), preferred_element_type=F32)


def _split_bf16(x):
    hi = x.astype(BF16)
    lo = (x - hi.astype(F32)).astype(BF16)
    return hi, lo


def _log_sigmoid(x):
    return jnp.minimum(x, 0.0) - jnp.log(1.0 + jnp.exp(-jnp.abs(x)))


def _inproj_kernel(x_ref, g_ref, w_ref, wlr_ref, wg_ref, bg_ref, p_ref, la_ref, *, col_chunk):
    x = x_ref[...]
    ms = jnp.mean(x * x, axis=-1, keepdims=True)
    y = (x * lax.rsqrt(ms + EPS) * g_ref[...]).astype(BF16)
    for c in range(0, P_COLS, col_chunk):
        p_ref[:, c:c + col_chunk] = _dot(y, w_ref[:, c:c + col_chunk]).astype(BF16)
    g_lr = _dot(y, wlr_ref[...])
    logit = _dot(g_lr.astype(BF16), wg_ref[...]) + bg_ref[...]
    la_ref[...] = _log_sigmoid(logit) * (1.0 / GLA_GATE_NORMALIZER)


def _inproj(x2d, g, w_main, w_lr, w_gate, b_gate, *, tm):
    T = x2d.shape[0]
    const = lambda i: (0, 0)
    return pl.pallas_call(
        functools.partial(_inproj_kernel, col_chunk=512),
        grid=(T // tm,),
        in_specs=[
            pl.BlockSpec((tm, D_MODEL), lambda i: (i, 0)),
            pl.BlockSpec((1, D_MODEL), const),
            pl.BlockSpec((D_MODEL, P_COLS), const),
            pl.BlockSpec((D_MODEL, LANES), const),
            pl.BlockSpec((LANES, GLA_KEY_WIDTH), const),
            pl.BlockSpec((1, GLA_KEY_WIDTH), const),
        ],
        out_specs=[
            pl.BlockSpec((tm, P_COLS), lambda i: (i, 0)),
            pl.BlockSpec((tm, GLA_KEY_WIDTH), lambda i: (i, 0)),
        ],
        out_shape=[
            jax.ShapeDtypeStruct((T, P_COLS), BF16),
            jax.ShapeDtypeStruct((T, GLA_KEY_WIDTH), F32),
        ],
        compiler_params=pltpu.CompilerParams(
            dimension_semantics=("arbitrary",), vmem_limit_bytes=VMEM_LIMIT_BYTES),
        name="inproj",
    )(x2d, g, w_main, w_lr, w_gate, b_gate)


def _sb_kernel(q_ref, k_ref, v_ref, g_ref, o_ref, kb_ref, vb_ref, acc_ref, carry_ref, *, tile):
    n_pairs = SB_HEADS // 2
    pairs = range(n_pairs)
    i = pl.program_id(1)
    low_half = lax.broadcasted_iota(jnp.int32, (1, LANES), 1) < SB_HEAD_DIM

    @pl.when(i == 0)
    def _():
        low = lax.broadcasted_iota(jnp.int32, (1, SB_WIDTH), 1) % LANES < SB_HEAD_DIM
        for src, dst in ((k_ref, kb_ref), (v_ref, vb_ref)):
            dst[0] = jnp.where(low, src[...], 0)
            dst[1] = jnp.where(low, 0, src[...])

    row = lax.broadcasted_iota(jnp.int32, (4 * tile, 2 * tile), 0) % (2 * tile)
    col = lax.broadcasted_iota(jnp.int32, (4 * tile, 2 * tile), 1)
    later2 = jnp.where((row > col) & ((row >= tile) == (col >= tile)), 1.0, 0.0).astype(BF16)
    t_idx = lax.broadcasted_iota(jnp.int32, (tile, 2 * tile), 0)
    s_idx = lax.broadcasted_iota(jnp.int32, (tile, 2 * tile), 1) % tile
    strict = s_idx < t_idx

    def both_heads(ref, keys, p):
        cols = slice(p * LANES, (p + 1) * LANES)
        return jnp.concatenate([ref[0, keys, cols], ref[1, keys, cols]], axis=0)

    def window(key_starts, diagonal_first):
        blocks = range(len(key_starts))
        keys = [pl.ds(ks, tile) for ks in key_starts]
        zn = [[_dot_nt(q_ref[:, p * LANES:(p + 1) * LANES], both_heads(kb_ref, keys[b], p))
               for p in pairs] for b in blocks]
        logsig, parts, first_col = [], [], []
        for b in blocks:
            ls_b, parts_b, fc_b = [], [], []
            for p in pairs:
                z = zn[b][p]
                neg_abs = pltpu.bitcast(pltpu.bitcast(z, jnp.uint32) | jnp.uint32(0x80000000), F32)
                log1m = jnp.minimum(z, 0.0) - jnp.log2(1.0 + jnp.exp2(neg_abs))
                ls_b.append(log1m - z)
                if diagonal_first and b == 0:
                    log1m = jnp.where(strict, log1m, 0.0)
                hi, lo = _split_bf16(log1m)
                parts_b.append(jnp.concatenate([hi, lo], axis=1))
                fc_b.append((log1m[:, 0:1], log1m[:, tile:tile + 1]))
            logsig.append(ls_b)
            parts.append(parts_b)
            first_col.append(fc_b)
        after = [[_dot(parts[b][p], later2) for p in pairs] for b in blocks]
        live = None
        for p in pairs:
            carry = None if diagonal_first else carry_ref[p]
            wb = []
            for b in blocks:
                aft = after[b][p]
                total = jnp.concatenate(
                    [jnp.broadcast_to(aft[:, 0:1] + first_col[b][p][0], (tile, tile)),
                     jnp.broadcast_to(aft[:, tile:tile + 1] + first_col[b][p][1], (tile, tile))], axis=1)
                if carry is not None:
                    aft = aft + carry
                    total = total + carry
                carry = total
                w = jnp.exp2(logsig[b][p] + aft)
                if diagonal_first and b == 0:
                    w = jnp.where(strict, w, 0.0)
                wb.append(w.astype(BF16))
            carry_ref[p] = carry
            live = carry if live is None else jnp.maximum(live, carry)
            pv = _dot(jnp.concatenate(wb, axis=1),
                      jnp.concatenate([both_heads(vb_ref, keys[b], p) for b in blocks], axis=0))
            if diagonal_first:
                acc_ref[p] = pv
            else:
                acc_ref[p] += pv
        return live

    q0 = i * tile

    @pl.when(i == 0)
    def _():
        window([q0], True)

    @pl.when(i == 1)
    def _():
        window([q0, q0 - tile], True)

    @pl.when(i >= 2)
    def _():
        window([q0, q0 - tile, q0 - 2 * tile], True)

    def cond(state):
        j, live = state
        return jnp.logical_and(j >= 0, live > SB_DEAD_LOG2_WEIGHT)

    def body(state):
        j, _ = state
        return j - 1, jnp.max(window([j * tile], False))

    live0 = carry_ref[0]
    for p in range(1, n_pairs):
        live0 = jnp.maximum(live0, carry_ref[p])
    lax.while_loop(cond, body, (i - 3, jnp.max(live0)))

    for p in pairs:
        o = acc_ref[p]
        o2 = o * o
        ss_lo = jnp.sum(jnp.where(low_half, o2, 0.0), axis=-1, keepdims=True)
        ss_hi = jnp.sum(jnp.where(low_half, 0.0, o2), axis=-1, keepdims=True)
        inv = jnp.where(low_half,
                        lax.rsqrt(ss_lo * (1.0 / SB_HEAD_DIM) + EPS),
                        lax.rsqrt(ss_hi * (1.0 / SB_HEAD_DIM) + EPS))
        o_ref[:, p * LANES:(p + 1) * LANES] = (o * inv * g_ref[:, p * LANES:(p + 1) * LANES]).astype(BF16)


def _sb_attention(p_act, g, *, batch, seq, tile):
    T = batch * seq
    nq = seq // tile
    n_pairs = SB_HEADS // 2
    return pl.pallas_call(
        functools.partial(_sb_kernel, tile=tile),
        grid=(batch, nq),
        in_specs=[
            pl.BlockSpec((tile, SB_WIDTH), lambda b, i: (b * nq + i, P_SB_Q // SB_WIDTH)),
            pl.BlockSpec((seq, SB_WIDTH), lambda b, i: (b, P_SB_K // SB_WIDTH)),
            pl.BlockSpec((seq, SB_WIDTH), lambda b, i: (b, P_SB_V // SB_WIDTH)),
            pl.BlockSpec((1, SB_WIDTH), lambda b, i: (0, 0)),
        ],
        out_specs=pl.BlockSpec((tile, SB_WIDTH), lambda b, i: (b * nq + i, 0)),
        out_shape=jax.ShapeDtypeStruct((T, SB_WIDTH), BF16),
        scratch_shapes=[
            pltpu.VMEM((2, seq, SB_WIDTH), BF16),
            pltpu.VMEM((2, seq, SB_WIDTH), BF16),
            pltpu.VMEM((n_pairs, tile, LANES), F32),
            pltpu.VMEM((n_pairs, tile, 2 * tile), F32),
        ],
        compiler_params=pltpu.CompilerParams(
            dimension_semantics=("arbitrary", "arbitrary"), vmem_limit_bytes=VMEM_LIMIT_BYTES),
        name="sb",
    )(p_act, p_act, p_act, g)


def _gla_kernel(q_ref, k_ref, v_ref, og_ref, la_ref, g_ref, o_ref, state_ref, *, seq):
    C = GLA_CHUNK
    n_pairs = GLA_HEADS // 2
    lane = lax.broadcasted_iota(jnp.int32, (1, LANES), 1)
    low_half = lane < GLA_DK
    ti = lax.broadcasted_iota(jnp.int32, (C, C), 0)
    tj = lax.broadcasted_iota(jnp.int32, (C, C), 1)
    causal = tj <= ti
    incl = jnp.where(causal, 1.0, 0.0).astype(BF16)
    state_ref[...] = jnp.zeros_like(state_ref)

    def chunk(c, carry):
        rows = pl.ds(pl.multiple_of(c * C, C), C)
        for p in range(n_pairs):
            ks = slice(p * LANES, (p + 1) * LANES)
            la_hi, la_lo = _split_bf16(la_ref[rows, ks])
            b = _dot(incl, la_hi) + _dot(incl, la_lo)
            b_last = b[C - 1:C, :]
            q = q_ref[rows, ks].astype(F32) * (GLA_DK ** -0.5)
            k = k_ref[rows, ks].astype(F32)
            q_dec = q * jnp.exp(b)
            k_inv = (k * jnp.exp(-b)).astype(BF16)
            k_end = k * jnp.exp(b_last - b)
            decay = jnp.exp(b_last)
            k_end_half = (jnp.where(low_half, k_end, 0.0).astype(BF16),
                          jnp.where(low_half, 0.0, k_end).astype(BF16))
            q_half = (jnp.where(low_half, q_dec, 0.0).astype(BF16),
                      jnp.where(low_half, 0.0, q_dec).astype(BF16))
            state_t = state_ref[p]
            state_b = state_t.astype(BF16)
            kv = None
            for hh in range(2):
                h = 2 * p + hh
                vs = slice(h * GLA_DV, (h + 1) * GLA_DV)
                v = v_ref[rows, vs]
                attn = jnp.where(causal, _dot_nt(q_half[hh], k_inv), 0.0)
                o = _dot(attn.astype(BF16), v) + _dot_nt(q_half[hh], state_b)
                inv = lax.rsqrt(jnp.mean(o * o, axis=-1, keepdims=True) + EPS)
                gate = og_ref[rows, vs].astype(F32)
                gate = gate / (1.0 + jnp.exp(-gate))
                o_ref[rows, vs] = (o * inv * g_ref[:, vs] * gate).astype(BF16)
                upd = _dot_tn(v, k_end_half[hh])
                kv = upd if kv is None else kv + upd
            state_ref[p] = state_t * decay + kv
        return carry

    lax.fori_loop(0, seq // C, chunk, 0)


def _gla(p_act, log_a, g, *, batch, seq):
    T = batch * seq
    n_pairs = GLA_HEADS // 2
    return pl.pallas_call(
        functools.partial(_gla_kernel, seq=seq),
        grid=(batch,),
        in_specs=[
            pl.BlockSpec((seq, GLA_KEY_WIDTH), lambda b: (b, P_G_Q // GLA_KEY_WIDTH)),
            pl.BlockSpec((seq, GLA_KEY_WIDTH), lambda b: (b, P_G_K // GLA_KEY_WIDTH)),
            pl.BlockSpec((seq, GLA_WIDTH), lambda b: (b, P_G_V // GLA_WIDTH)),
            pl.BlockSpec((seq, GLA_WIDTH), lambda b: (b, P_G_OG // GLA_WIDTH)),
            pl.BlockSpec((seq, GLA_KEY_WIDTH), lambda b: (b, 0)),
            pl.BlockSpec((1, GLA_WIDTH), lambda b: (0, 0)),
        ],
        out_specs=pl.BlockSpec((seq, GLA_WIDTH), lambda b: (b, 0)),
        out_shape=jax.ShapeDtypeStruct((T, GLA_WIDTH), BF16),
        scratch_shapes=[pltpu.VMEM((n_pairs, GLA_DV, LANES), F32)],
        compiler_params=pltpu.CompilerParams(
            dimension_semantics=("arbitrary",), vmem_limit_bytes=VMEM_LIMIT_BYTES),
        name="gla",
    )(p_act, p_act, p_act, p_act, log_a, g)


def _rms(x, g):
    ms = jnp.mean(x * x, axis=-1, keepdims=True)
    return x * lax.rsqrt(ms + EPS) * g


def _ffn_kernel(x_ref, osb_ref, ogla_ref, wo_sb_ref, wo_gla_ref, g2_ref, wup_ref, cw_ref, cb_ref,
                wdn_ref, gf_ref, out_ref, x1_ref, h2_ref, acc_ref, u00_ref, u01_ref, u10_ref, u11_ref,
                halo_ref,
                *, tm, fc, tiles_per_seq):
    t = pl.program_id(0)
    n_chunks = D_FF // fc
    H = SUBLANES

    @pl.when(t % tiles_per_seq == 0)
    def _():
        halo_ref[...] = jnp.zeros_like(halo_ref)

    x1 = x_ref[...] + _dot(osb_ref[...], wo_sb_ref[...]) + _dot(ogla_ref[...], wo_gla_ref[...])
    x1_ref[...] = x1
    h2_ref[...] = _rms(x1, g2_ref[...]).astype(BF16)

    u_bufs = ((u00_ref, u01_ref), (u10_ref, u11_ref))

    def cols_of(part, c):
        return slice(part * D_FF + c * fc, part * D_FF + (c + 1) * fc)

    def up_project(c, part):
        buf = u_bufs[c % 2][part]
        u = _dot(h2_ref[...], wup_ref[:, cols_of(part, c)])
        buf[0:H, :] = halo_ref[part, c]
        buf[H:H + tm, :] = u
        halo_ref[part, c] = u[tm - H:tm, :]

    def conv_gate_down(c, r0, nrows):
        conv = []
        for part in range(2):
            buf = u_bufs[c % 2][part]
            cw = cw_ref[:, cols_of(part, c)]
            conv.append(cb_ref[:, cols_of(part, c)] + cw[2:3, :] * buf[H + r0:H + r0 + nrows, :]
                        + cw[1:2, :] * buf[H - 1 + r0:H - 1 + r0 + nrows, :]
                        + cw[0:1, :] * buf[H - 2 + r0:H - 2 + r0 + nrows, :])
        a, val = conv
        gated = (a / (1.0 + jnp.exp(-a)) * val).astype(BF16)
        down = _dot(gated, wdn_ref[c * fc:(c + 1) * fc, :])
        if c == 0:
            acc_ref[r0:r0 + nrows, :] = down
        else:
            acc_ref[r0:r0 + nrows, :] += down

    half = tm // 2
    up_project(0, 0)
    up_project(0, 1)
    for c in range(n_chunks):
        for part in range(2):
            if c + 1 < n_chunks:
                up_project(c + 1, part)
            conv_gate_down(c, part * half, half)

    out_ref[...] = _rms(x1_ref[...] + acc_ref[...], gf_ref[...])


def _ffn(x2d, o_sb, o_gla, wo_sb, wo_gla, g2, w_up, conv_w, conv_b, w_down, gf, *, tm, fc, seq):
    T = x2d.shape[0]
    n_chunks = D_FF // fc
    const = lambda i: (0, 0)
    resident = functools.partial(pl.BlockSpec, index_map=const, pipeline_mode=pl.Buffered(1))
    return pl.pallas_call(
        functools.partial(_ffn_kernel, tm=tm, fc=fc, tiles_per_seq=seq // tm),
        grid=(T // tm,),
        in_specs=[
            pl.BlockSpec((tm, D_MODEL), lambda i: (i, 0)),
            pl.BlockSpec((tm, SB_WIDTH), lambda i: (i, 0)),
            pl.BlockSpec((tm, GLA_WIDTH), lambda i: (i, 0)),
            resident((SB_WIDTH, D_MODEL)),
            resident((GLA_WIDTH, D_MODEL)),
            resident((1, D_MODEL)),
            resident((D_MODEL, 2 * D_FF)),
            resident((CONV_WIDTH, 2 * D_FF)),
            resident((1, 2 * D_FF)),
            resident((D_FF, D_MODEL)),
            resident((1, D_MODEL)),
        ],
        out_specs=pl.BlockSpec((tm, D_MODEL), lambda i: (i, 0)),
        out_shape=jax.ShapeDtypeStruct((T, D_MODEL), F32),
        scratch_shapes=[
            pltpu.VMEM((tm, D_MODEL), F32),
            pltpu.VMEM((tm, D_MODEL), BF16),
            pltpu.VMEM((tm, D_MODEL), F32),
            pltpu.VMEM((SUBLANES + tm, fc), F32),
            pltpu.VMEM((SUBLANES + tm, fc), F32),
            pltpu.VMEM((SUBLANES + tm, fc), F32),
            pltpu.VMEM((SUBLANES + tm, fc), F32),
            pltpu.VMEM((2, n_chunks, SUBLANES, fc), F32),
        ],
        compiler_params=pltpu.CompilerParams(
            dimension_semantics=("arbitrary",), vmem_limit_bytes=VMEM_LIMIT_BYTES),
        name="ffn",
    )(x2d, o_sb, o_gla, wo_sb, wo_gla, g2, w_up, conv_w, conv_b, w_down, gf)


def kernel(x, attn_norm_g, w_in, w_gate_up, b_gate_up, sb_out_g, gla_out_g, w_out, ffn_norm_g,
           w_ffn_up, conv_w, conv_b, w_ffn_down, final_norm_g):
    batch, seq, d_model = x.shape
    assert d_model == D_MODEL and attn_norm_g.shape[0] == 1
    T = batch * seq
    x2d = x.reshape(T, D_MODEL)

    lr0 = P_G_OG
    w_in0 = w_in[0]
    sb_q_scale = -(SB_HEAD_DIM ** -0.5) * LOG2_E
    w_main = jnp.concatenate([w_in0[:, :SB_WIDTH] * sb_q_scale, w_in0[:, SB_WIDTH:lr0],
                              w_in0[:, lr0 + GLA_GATE_RANK:]], axis=1).astype(BF16)
    w_lr = jnp.pad(w_in0[:, lr0:lr0 + GLA_GATE_RANK], ((0, 0), (0, LANES - GLA_GATE_RANK))).astype(BF16)
    w_gate = jnp.pad(w_gate_up[0], ((0, LANES - GLA_GATE_RANK), (0, 0))).astype(BF16)

    tm = min(512, seq)
    p_act, log_a = _inproj(x2d, attn_norm_g, w_main, w_lr, w_gate, b_gate_up, tm=tm)
    o_sb = _sb_attention(p_act, sb_out_g, batch=batch, seq=seq, tile=LANES)
    o_gla = _gla(p_act, log_a, gla_out_g, batch=batch, seq=seq)
    out = _ffn(x2d, o_sb, o_gla,
               w_out[0, :SB_WIDTH].astype(BF16), w_out[0, SB_WIDTH:].astype(BF16),
               ffn_norm_g, w_ffn_up[0].astype(BF16), conv_w[0], conv_b, w_ffn_down[0].astype(BF16),
               final_norm_g.reshape(1, D_MODEL), tm=tm, fc=256, seq=seq)
    return out.reshape(batch, seq, D_MODEL)
```

```python
import functools

import jax
import jax.numpy as jnp
from jax import lax
from jax.experimental import pallas as pl
from jax.experimental.pallas import tpu as pltpu

F32 = jnp.float32
BF16 = jnp.bfloat16

D_MODEL = 1024
SB_HEAD_DIM = 64
SB_WIDTH = 512
SB_HEADS = SB_WIDTH // SB_HEAD_DIM
GLA_HEADS = 4
GLA_WIDTH = 512
GLA_DV = GLA_WIDTH // GLA_HEADS
GLA_DK = GLA_DV // 2
GLA_KEY_WIDTH = GLA_HEADS * GLA_DK
GLA_GATE_RANK = 16
GLA_GATE_NORMALIZER = 16.0
GLA_CHUNK = 64
D_FF = 2816
CONV_WIDTH = 3
EPS = 1e-6

LANES = 128
SUBLANES = 8
VMEM_LIMIT_BYTES = 56 * 1024 * 1024

P_SB_Q, P_SB_K, P_SB_V = 0, 512, 1024
P_G_Q, P_G_K, P_G_V, P_G_OG = 1536, 1792, 2048, 2560
P_COLS = 3072

LOG2_E = 1.4426950408889634
SB_DEAD_LOG2_WEIGHT = -127.0


def _dot(a, b):
    return jnp.dot(a, b, preferred_element_type=F32)


def _dot_nt(a, b):
    return lax.dot_general(a, b, (((1,), (1,)), ((), ())), preferred_element_type=F32)


def _dot_tn(a, b):
    return lax.dot_general(a, b, (((0,), (0,)), ((), ())), preferred_element_type=F32)


def _split_bf16(x):
    hi = x.astype(BF16)
    lo = (x - hi.astype(F32)).astype(BF16)
    return hi, lo


def _log_sigmoid(x):
    return jnp.minimum(x, 0.0) - jnp.log(1.0 + jnp.exp(-jnp.abs(x)))


def _inproj_kernel(x_ref, g_ref, w_ref, wlr_ref, wg_ref, bg_ref, p_ref, la_ref, *, col_chunk):
    x = x_ref[...]
    ms = jnp.mean(x * x, axis=-1, keepdims=True)
    y = (x * lax.rsqrt(ms + EPS) * g_ref[...]).astype(BF16)
    for c in range(0, P_COLS, col_chunk):
        p_ref[:, c:c + col_chunk] = _dot(y, w_ref[:, c:c + col_chunk]).astype(BF16)
    g_lr = _dot(y, wlr_ref[...])
    logit = _dot(g_lr.astype(BF16), wg_ref[...]) + bg_ref[...]
    la_ref[...] = _log_sigmoid(logit) * (1.0 / GLA_GATE_NORMALIZER)


def _inproj(x2d, g, w_main, w_lr, w_gate, b_gate, *, tm):
    T = x2d.shape[0]
    const = lambda i: (0, 0)
    return pl.pallas_call(
        functools.partial(_inproj_kernel, col_chunk=512),
        grid=(T // tm,),
        in_specs=[
            pl.BlockSpec((tm, D_MODEL), lambda i: (i, 0)),
            pl.BlockSpec((1, D_MODEL), const),
            pl.BlockSpec((D_MODEL, P_COLS), const),
            pl.BlockSpec((D_MODEL, LANES), const),
            pl.BlockSpec((LANES, GLA_KEY_WIDTH), const),
            pl.BlockSpec((1, GLA_KEY_WIDTH), const),
        ],
        out_specs=[
            pl.BlockSpec((tm, P_COLS), lambda i: (i, 0)),
            pl.BlockSpec((tm, GLA_KEY_WIDTH), lambda i: (i, 0)),
        ],
        out_shape=[
            jax.ShapeDtypeStruct((T, P_COLS), BF16),
            jax.ShapeDtypeStruct((T, GLA_KEY_WIDTH), F32),
        ],
        compiler_params=pltpu.CompilerParams(
            dimension_semantics=("arbitrary",), vmem_limit_bytes=VMEM_LIMIT_BYTES),
        name="inproj",
    )(x2d, g, w_main, w_lr, w_gate, b_gate)


def _sb_kernel(q_ref, k_ref, v_ref, g_ref, o_ref, kb_ref, vb_ref, acc_ref, carry_ref, *, tile):
    n_pairs = SB_HEADS // 2
    pairs = range(n_pairs)
    i = pl.program_id(1)
    low_half = lax.broadcasted_iota(jnp.int32, (1, LANES), 1) < SB_HEAD_DIM

    @pl.when(i == 0)
    def _():
        low = lax.broadcasted_iota(jnp.int32, (1, SB_WIDTH), 1) % LANES < SB_HEAD_DIM
        for src, dst in ((k_ref, kb_ref), (v_ref, vb_ref)):
            dst[0] = jnp.where(low, src[...], 0)
            dst[1] = jnp.where(low, 0, src[...])

    row = lax.broadcasted_iota(jnp.int32, (4 * tile, 2 * tile), 0) % (2 * tile)
    col = lax.broadcasted_iota(jnp.int32, (4 * tile, 2 * tile), 1)
    later2 = jnp.where((row > col) & ((row >= tile) == (col >= tile)), 1.0, 0.0).astype(BF16)
    t_idx = lax.broadcasted_iota(jnp.int32, (tile, 2 * tile), 0)
    s_idx = lax.broadcasted_iota(jnp.int32, (tile, 2 * tile), 1) % tile
    strict = s_idx < t_idx

    def both_heads(ref, keys, p):
        cols = slice(p * LANES, (p + 1) * LANES)
        return jnp.concatenate([ref[0, keys, cols], ref[1, keys, cols]], axis=0)

    def window(key_starts, diagonal_first):
        n_blocks = len(key_starts)
        keys = [pl.ds(ks, tile) for ks in key_starts]
        units = [(p, b) for p in pairs for b in range(n_blocks)]
        zn, logsig, first_col, after = {}, {}, {}, {}
        state = {"carry": None, "pv": None, "live": None}

        def scores(u):
            p, b = u
            zn[u] = _dot_nt(q_ref[:, p * LANES:(p + 1) * LANES], both_heads(kb_ref, keys[b], p))

        def log_terms(u):
            p, b = u
            z = zn.pop(u)
            log1m = jnp.minimum(z, 0.0) - jnp.log2(1.0 + jnp.exp2(-jnp.abs(z)))
            logsig[u] = log1m - z
            if diagonal_first and b == 0:
                log1m = jnp.where(strict, log1m, 0.0)
            hi, lo = _split_bf16(log1m)
            first_col[u] = (log1m[:, 0:1], log1m[:, tile:tile + 1])
            after[u] = _dot(jnp.concatenate([hi, lo], axis=1), later2)

        def weights_and_values(u):
            p, b = u
            if b == 0:
                state["carry"] = None if diagonal_first else carry_ref[p]
                state["pv"] = None
            carry = state["carry"]
            aft = after.pop(u)
            fc = first_col.pop(u)
            total = jnp.concatenate(
                [jnp.broadcast_to(aft[:, 0:1] + fc[0], (tile, tile)),
                 jnp.broadcast_to(aft[:, tile:tile + 1] + fc[1], (tile, tile))], axis=1)
            if carry is not None:
                aft = aft + carry
                total = total + carry
            state["carry"] = total
            w = jnp.exp2(logsig.pop(u) + aft)
            if diagonal_first and b == 0:
                w = jnp.where(strict, w, 0.0)
            pv = _dot(w.astype(BF16), both_heads(vb_ref, keys[b], p))
            state["pv"] = pv if state["pv"] is None else state["pv"] + pv
            if b == n_blocks - 1:
                carry_ref[p] = total
                state["live"] = total if state["live"] is None else jnp.maximum(state["live"], total)
                if diagonal_first:
                    acc_ref[p] = state["pv"]
                else:
                    acc_ref[p] += state["pv"]

        lag = 2
        n = len(units)
        for t in range(n + 2 * lag):
            if t < n:
                scores(units[t])
            if 0 <= t - lag < n:
                log_terms(units[t - lag])
            if 0 <= t - 2 * lag < n:
                weights_and_values(units[t - 2 * lag])
        return state["live"]

    q0 = i * tile

    @pl.when(i == 0)
    def _():
        window([q0], True)

    @pl.when(i == 1)
    def _():
        window([q0, q0 - tile], True)

    @pl.when(i >= 2)
    def _():
        window([q0, q0 - tile, q0 - 2 * tile], True)

    def cond(state):
        j, live = state
        return jnp.logical_and(j >= 0, live > SB_DEAD_LOG2_WEIGHT)

    def body(state):
        j, _ = state
        return j - 1, jnp.max(window([j * tile], False))

    live0 = carry_ref[0]
    for p in range(1, n_pairs):
        live0 = jnp.maximum(live0, carry_ref[p])
    lax.while_loop(cond, body, (i - 3, jnp.max(live0)))

    for p in pairs:
        o = acc_ref[p]
        o2 = o * o
        ss_lo = jnp.sum(jnp.where(low_half, o2, 0.0), axis=-1, keepdims=True)
        ss_hi = jnp.sum(jnp.where(low_half, 0.0, o2), axis=-1, keepdims=True)
        inv = jnp.where(low_half,
                        lax.rsqrt(ss_lo * (1.0 / SB_HEAD_DIM) + EPS),
                        lax.rsqrt(ss_hi * (1.0 / SB_HEAD_DIM) + EPS))
        o_ref[:, p * LANES:(p + 1) * LANES] = (o * inv * g_ref[:, p * LANES:(p + 1) * LANES]).astype(BF16)


def _sb_attention(p_act, g, *, batch, seq, tile):
    T = batch * seq
    nq = seq // tile
    n_pairs = SB_HEADS // 2
    return pl.pallas_call(
        functools.partial(_sb_kernel, tile=tile),
        grid=(batch, nq),
        in_specs=[
            pl.BlockSpec((tile, SB_WIDTH), lambda b, i: (b * nq + i, P_SB_Q // SB_WIDTH)),
            pl.BlockSpec((seq, SB_WIDTH), lambda b, i: (b, P_SB_K // SB_WIDTH)),
            pl.BlockSpec((seq, SB_WIDTH), lambda b, i: (b, P_SB_V // SB_WIDTH)),
            pl.BlockSpec((1, SB_WIDTH), lambda b, i: (0, 0)),
        ],
        out_specs=pl.BlockSpec((tile, SB_WIDTH), lambda b, i: (b * nq + i, 0)),
        out_shape=jax.ShapeDtypeStruct((T, SB_WIDTH), BF16),
        scratch_shapes=[
            pltpu.VMEM((2, seq, SB_WIDTH), BF16),
            pltpu.VMEM((2, seq, SB_WIDTH), BF16),
            pltpu.VMEM((n_pairs, tile, LANES), F32),
            pltpu.VMEM((n_pairs, tile, 2 * tile), F32),
        ],
        compiler_params=pltpu.CompilerParams(
            dimension_semantics=("arbitrary", "arbitrary"), vmem_limit_bytes=VMEM_LIMIT_BYTES),
        name="sb",
    )(p_act, p_act, p_act, g)


def _gla_kernel(q_ref, k_ref, v_ref, og_ref, la_ref, g_ref, o_ref, state_ref, *, seq, step_rows):
    C = GLA_CHUNK
    R = step_rows
    n_chunks = R // C
    pairs = range(GLA_HEADS // 2)
    low_half = lax.broadcasted_iota(jnp.int32, (1, LANES), 1) < GLA_DK
    ri = lax.broadcasted_iota(jnp.int32, (R, R), 0)
    ci = lax.broadcasted_iota(jnp.int32, (R, R), 1)
    same_chunk = (ri // C) == (ci // C)
    causal = same_chunk & (ci <= ri)
    prefix = jnp.where(causal, 1.0, 0.0).astype(BF16)
    whole = jnp.where(same_chunk, 1.0, 0.0).astype(BF16)
    sums_lhs = jnp.concatenate([jnp.concatenate([prefix, prefix], axis=1),
                                jnp.concatenate([whole, whole], axis=1)], axis=0)
    state_ref[...] = jnp.zeros_like(state_ref)

    def step(s, carry):
        rows = pl.ds(pl.multiple_of(s * R, R), R)
        la_hi, la_lo = _split_bf16(la_ref[rows, :])
        sums = _dot(sums_lhs, jnp.concatenate([la_hi, la_lo], axis=0))
        b, b_end = sums[:R], sums[R:]
        q = q_ref[rows, :].astype(F32) * (GLA_DK ** -0.5)
        k = k_ref[rows, :].astype(F32)
        q_dec = q * jnp.exp(b)
        k_inv = (k * jnp.exp(-b)).astype(BF16)
        k_end = k * jnp.exp(b_end - b)
        decay = jnp.exp(b_end)
        q_half, k_end_half = [], []
        for p in pairs:
            ks = slice(p * LANES, (p + 1) * LANES)
            q_half.append((jnp.where(low_half, q_dec[:, ks], 0.0).astype(BF16),
                           jnp.where(low_half, 0.0, q_dec[:, ks]).astype(BF16)))
            k_end_half.append((jnp.where(low_half, k_end[:, ks], 0.0).astype(BF16),
                               jnp.where(low_half, 0.0, k_end[:, ks]).astype(BF16)))
        v = [v_ref[rows, h * GLA_DV:(h + 1) * GLA_DV] for h in range(GLA_HEADS)]
        attn = [jnp.where(causal, _dot_nt(q_half[h // 2][h % 2], k_inv[:, (h // 2) * LANES:(h // 2 + 1) * LANES]),
                          0.0).astype(BF16) for h in range(GLA_HEADS)]
        o_intra = [_dot(attn[h], v[h]) for h in range(GLA_HEADS)]
        kv = [[_dot_tn(jnp.concatenate([v[2 * p][c * C:(c + 1) * C], v[2 * p + 1][c * C:(c + 1) * C]], axis=0),
                       jnp.concatenate([k_end_half[p][0][c * C:(c + 1) * C],
                                        k_end_half[p][1][c * C:(c + 1) * C]], axis=0))
               for c in range(n_chunks)] for p in pairs]
        o_inter = [[] for _ in range(GLA_HEADS)]
        for p in pairs:
            state_t = state_ref[p]
            for c in range(n_chunks):
                state_b = state_t.astype(BF16)
                for hh in range(2):
                    o_inter[2 * p + hh].append(_dot_nt(q_half[p][hh][c * C:(c + 1) * C], state_b))
                state_t = state_t * decay[c * C:c * C + 1, p * LANES:(p + 1) * LANES] + kv[p][c]
            state_ref[p] = state_t
        for h in range(GLA_HEADS):
            vs = slice(h * GLA_DV, (h + 1) * GLA_DV)
            o = o_intra[h] + jnp.concatenate(o_inter[h], axis=0)
            inv = lax.rsqrt(jnp.mean(o * o, axis=-1, keepdims=True) + EPS)
            gate = og_ref[rows, vs].astype(F32)
            gate = gate / (1.0 + jnp.exp(-gate))
            o_ref[rows, vs] = (o * inv * g_ref[:, vs] * gate).astype(BF16)
        return carry

    lax.fori_loop(0, seq // R, step, 0, unroll=2)


def _gla(p_act, log_a, g, *, batch, seq):
    T = batch * seq
    n_pairs = GLA_HEADS // 2
    return pl.pallas_call(
        functools.partial(_gla_kernel, seq=seq, step_rows=min(256, seq)),
        grid=(batch,),
        in_specs=[
            pl.BlockSpec((seq, GLA_KEY_WIDTH), lambda b: (b, P_G_Q // GLA_KEY_WIDTH)),
            pl.BlockSpec((seq, GLA_KEY_WIDTH), lambda b: (b, P_G_K // GLA_KEY_WIDTH)),
            pl.BlockSpec((seq, GLA_WIDTH), lambda b: (b, P_G_V // GLA_WIDTH)),
            pl.BlockSpec((seq, GLA_WIDTH), lambda b: (b, P_G_OG // GLA_WIDTH)),
            pl.BlockSpec((seq, GLA_KEY_WIDTH), lambda b: (b, 0)),
            pl.BlockSpec((1, GLA_WIDTH), lambda b: (0, 0)),
        ],
        out_specs=pl.BlockSpec((seq, GLA_WIDTH), lambda b: (b, 0)),
        out_shape=jax.ShapeDtypeStruct((T, GLA_WIDTH), BF16),
        scratch_shapes=[pltpu.VMEM((n_pairs, GLA_DV, LANES), F32)],
        compiler_params=pltpu.CompilerParams(
            dimension_semantics=("arbitrary",), vmem_limit_bytes=VMEM_LIMIT_BYTES),
        name="gla",
    )(p_act, p_act, p_act, p_act, log_a, g)


def _rms(x, g):
    ms = jnp.mean(x * x, axis=-1, keepdims=True)
    return x * lax.rsqrt(ms + EPS) * g


def _ffn_kernel(x_ref, osb_ref, ogla_ref, wo_sb_ref, wo_gla_ref, g2_ref, wup_ref, cw_ref, cb_ref,
                wdn_ref, gf_ref, out_ref, x1_ref, h2_ref, acc_ref, u00_ref, u01_ref, u10_ref, u11_ref,
                halo_ref,
                *, tm, fc, tiles_per_seq):
    t = pl.program_id(0)
    n_chunks = D_FF // fc
    H = SUBLANES

    @pl.when(t % tiles_per_seq == 0)
    def _():
        halo_ref[...] = jnp.zeros_like(halo_ref)

    x1 = x_ref[...] + _dot(osb_ref[...], wo_sb_ref[...]) + _dot(ogla_ref[...], wo_gla_ref[...])
    x1_ref[...] = x1
    h2_ref[...] = _rms(x1, g2_ref[...]).astype(BF16)

    u_bufs = ((u00_ref, u01_ref), (u10_ref, u11_ref))

    def cols_of(part, c):
        return slice(part * D_FF + c * fc, part * D_FF + (c + 1) * fc)

    def up_project(c, part):
        buf = u_bufs[c % 2][part]
        u = _dot(h2_ref[...], wup_ref[:, cols_of(part, c)])
        buf[0:H, :] = halo_ref[part, c]
        buf[H:H + tm, :] = u
        halo_ref[part, c] = u[tm - H:tm, :]

    def conv_gate_down(c, r0, nrows):
        conv = []
        for part in range(2):
            buf = u_bufs[c % 2][part]
            cw = cw_ref[:, cols_of(part, c)]
            conv.append(cb_ref[:, cols_of(part, c)] + cw[2:3, :] * buf[H + r0:H + r0 + nrows, :]
                        + cw[1:2, :] * buf[H - 1 + r0:H - 1 + r0 + nrows, :]
                        + cw[0:1, :] * buf[H - 2 + r0:H - 2 + r0 + nrows, :])
        a, val = conv
        gated = (a / (1.0 + jnp.exp(-a)) * val).astype(BF16)
        down = _dot(gated, wdn_ref[c * fc:(c + 1) * fc, :])
        if c == 0:
            acc_ref[r0:r0 + nrows, :] = down
        else:
            acc_ref[r0:r0 + nrows, :] += down

    half = tm // 2
    up_project(0, 0)
    up_project(0, 1)
    for c in range(n_chunks):
        for part in range(2):
            if c + 1 < n_chunks:
                up_project(c + 1, part)
            conv_gate_down(c, part * half, half)

    out_ref[...] = _rms(x1_ref[...] + acc_ref[...], gf_ref[...])


def _ffn(x2d, o_sb, o_gla, wo_sb, wo_gla, g2, w_up, conv_w, conv_b, w_down, gf, *, tm, fc, seq):
    T = x2d.shape[0]
    n_chunks = D_FF // fc
    const = lambda i: (0, 0)
    resident = functools.partial(pl.BlockSpec, index_map=const, pipeline_mode=pl.Buffered(1))
    return pl.pallas_call(
        functools.partial(_ffn_kernel, tm=tm, fc=fc, tiles_per_seq=seq // tm),
        grid=(T // tm,),
        in_specs=[
            pl.BlockSpec((tm, D_MODEL), lambda i: (i, 0)),
            pl.BlockSpec((tm, SB_WIDTH), lambda i: (i, 0)),
            pl.BlockSpec((tm, GLA_WIDTH), lambda i: (i, 0)),
            resident((SB_WIDTH, D_MODEL)),
            resident((GLA_WIDTH, D_MODEL)),
            resident((1, D_MODEL)),
            resident((D_MODEL, 2 * D_FF)),
            resident((CONV_WIDTH, 2 * D_FF)),
            resident((1, 2 * D_FF)),
            resident((D_FF, D_MODEL)),
            resident((1, D_MODEL)),
        ],
        out_specs=pl.BlockSpec((tm, D_MODEL), lambda i: (i, 0)),
        out_shape=jax.ShapeDtypeStruct((T, D_MODEL), F32),
        scratch_shapes=[
            pltpu.VMEM((tm, D_MODEL), F32),
            pltpu.VMEM((tm, D_MODEL), BF16),
            pltpu.VMEM((tm, D_MODEL), F32),
            pltpu.VMEM((SUBLANES + tm, fc), F32),
            pltpu.VMEM((SUBLANES + tm, fc), F32),
            pltpu.VMEM((SUBLANES + tm, fc), F32),
            pltpu.VMEM((SUBLANES + tm, fc), F32),
            pltpu.VMEM((2, n_chunks, SUBLANES, fc), F32),
        ],
        compiler_params=pltpu.CompilerParams(
            dimension_semantics=("arbitrary",), vmem_limit_bytes=VMEM_LIMIT_BYTES),
        name="ffn",
    )(x2d, o_sb, o_gla, wo_sb, wo_gla, g2, w_up, conv_w, conv_b, w_down, gf)


def kernel(x, attn_norm_g, w_in, w_gate_up, b_gate_up, sb_out_g, gla_out_g, w_out, ffn_norm_g,
           w_ffn_up, conv_w, conv_b, w_ffn_down, final_norm_g):
    batch, seq, d_model = x.shape
    assert d_model == D_MODEL and attn_norm_g.shape[0] == 1
    T = batch * seq
    x2d = x.reshape(T, D_MODEL)

    lr0 = P_G_OG
    w_in0 = w_in[0]
    sb_q_scale = -(SB_HEAD_DIM ** -0.5) * LOG2_E
    w_main = jnp.concatenate([w_in0[:, :SB_WIDTH] * sb_q_scale, w_in0[:, SB_WIDTH:lr0],
                              w_in0[:, lr0 + GLA_GATE_RANK:]], axis=1).astype(BF16)
    w_lr = jnp.pad(w_in0[:, lr0:lr0 + GLA_GATE_RANK], ((0, 0), (0, LANES - GLA_GATE_RANK))).astype(BF16)
    w_gate = jnp.pad(w_gate_up[0], ((0, LANES - GLA_GATE_RANK), (0, 0))).astype(BF16)

    tm = min(512, seq)
    p_act, log_a = _inproj(x2d, attn_norm_g, w_main, w_lr, w_gate, b_gate_up, tm=tm)
    o_sb = _sb_attention(p_act, sb_out_g, batch=batch, seq=seq, tile=LANES)
    o_gla = _gla(p_act, log_a, gla_out_g, batch=batch, seq=seq)
    out = _ffn(x2d, o_sb, o_gla,
               w_out[0, :SB_WIDTH].astype(BF16), w_out[0, SB_WIDTH:].astype(BF16),
               ffn_norm_g, w_ffn_up[0].astype(BF16), conv_w[0], conv_b, w_ffn_down[0].astype(BF16),
               final_norm_g.reshape(1, D_MODEL), tm=tm, fc=256, seq=seq)
    return out.reshape(batch, seq, D_MODEL)
```

```python
import functools

import jax
import jax.numpy as jnp
from jax import lax
from jax.experimental import pallas as pl
from jax.experimental.pallas import tpu as pltpu

F32 = jnp.float32
BF16 = jnp.bfloat16

D_MODEL = 1024
SB_HEAD_DIM = 64
SB_WIDTH = 512
SB_HEADS = SB_WIDTH // SB_HEAD_DIM
GLA_HEADS = 4
GLA_WIDTH = 512
GLA_DV = GLA_WIDTH // GLA_HEADS
GLA_DK = GLA_DV // 2
GLA_KEY_WIDTH = GLA_HEADS * GLA_DK
GLA_GATE_RANK = 16
GLA_GATE_NORMALIZER = 16.0
GLA_CHUNK = 64
D_FF = 2816
CONV_WIDTH = 3
EPS = 1e-6

LANES = 128
SUBLANES = 8
VMEM_LIMIT_BYTES = 56 * 1024 * 1024

P_SB_Q, P_SB_K, P_SB_V = 0, 512, 1024
P_G_Q, P_G_K, P_G_V, P_G_OG = 1536, 1792, 2048, 2560
P_COLS = 3072

LOG2_E = 1.4426950408889634
SB_DEAD_LOG2_WEIGHT = -127.0
SB_MASKED_SCORE = 1.0e4


def _dot(a, b):
    return jnp.dot(a, b, preferred_element_type=F32)


def _dot_nt(a, b):
    return lax.dot_general(a, b, (((1,), (1,)), ((), ())), preferred_element_type=F32)


def _dot_tn(a, b):
    return lax.dot_general(a, b, (((0,), (0,)), ((), ())), preferred_element_type=F32)


def _split_bf16(x):
    hi = x.astype(BF16)
    lo = (x - hi.astype(F32)).astype(BF16)
    return hi, lo


def _log_sigmoid(x):
    return jnp.minimum(x, 0.0) - jnp.log(1.0 + jnp.exp(-jnp.abs(x)))


def _inproj_kernel(x_ref, g_ref, w_ref, wlr_ref, wg_ref, bg_ref, p_ref, la_ref, *, col_chunk, row_parts):
    part = x_ref.shape[0] // row_parts
    rows = [slice(r * part, (r + 1) * part) for r in range(row_parts)]
    y = {}
    for c in range(0, P_COLS, col_chunk):
        for r in range(row_parts):
            if r not in y:
                x = x_ref[rows[r], :]
                ms = jnp.mean(x * x, axis=-1, keepdims=True)
                y[r] = (x * lax.rsqrt(ms + EPS) * g_ref[...]).astype(BF16)
            p_ref[rows[r], c:c + col_chunk] = _dot(y[r], w_ref[:, c:c + col_chunk]).astype(BF16)
            if c == col_chunk:
                g_lr = _dot(y[r], wlr_ref[...])
                logit = _dot(g_lr.astype(BF16), wg_ref[...]) + bg_ref[...]
                la_ref[rows[r], :] = _log_sigmoid(logit) * (1.0 / GLA_GATE_NORMALIZER)


def _inproj(x2d, g, w_main, w_lr, w_gate, b_gate, *, tm):
    T = x2d.shape[0]
    const = lambda i: (0, 0)
    return pl.pallas_call(
        functools.partial(_inproj_kernel, col_chunk=512, row_parts=2),
        grid=(T // tm,),
        in_specs=[
            pl.BlockSpec((tm, D_MODEL), lambda i: (i, 0)),
            pl.BlockSpec((1, D_MODEL), const),
            pl.BlockSpec((D_MODEL, P_COLS), const),
            pl.BlockSpec((D_MODEL, LANES), const),
            pl.BlockSpec((LANES, GLA_KEY_WIDTH), const),
            pl.BlockSpec((1, GLA_KEY_WIDTH), const),
        ],
        out_specs=[
            pl.BlockSpec((tm, P_COLS), lambda i: (i, 0)),
            pl.BlockSpec((tm, GLA_KEY_WIDTH), lambda i: (i, 0)),
        ],
        out_shape=[
            jax.ShapeDtypeStruct((T, P_COLS), BF16),
            jax.ShapeDtypeStruct((T, GLA_KEY_WIDTH), F32),
        ],
        compiler_params=pltpu.CompilerParams(
            dimension_semantics=("arbitrary",), vmem_limit_bytes=VMEM_LIMIT_BYTES),
        name="inproj",
    )(x2d, g, w_main, w_lr, w_gate, b_gate)


def _sb_constants(tile):
    row = jnp.arange(4 * tile)[:, None] % (2 * tile)
    col = jnp.arange(2 * tile)[None, :]
    later2 = ((row > col) & ((row >= tile) == (col >= tile))).astype(BF16)
    t_idx = jnp.arange(tile)[:, None]
    s_idx = jnp.arange(2 * tile)[None, :] % tile
    not_causal = jnp.where(s_idx < t_idx, -SB_MASKED_SCORE * 1e30, SB_MASKED_SCORE).astype(F32)
    r = jnp.arange(2 * LANES)[:, None] % LANES
    c = jnp.arange(LANES)[None, :]
    head_ones = ((r // SB_HEAD_DIM) == (c // SB_HEAD_DIM)).astype(BF16)
    return later2, not_causal, head_ones


def _sb_kernel(q_ref, k_ref, v_ref, g_ref, later2_ref, not_causal_ref, head_ones_ref, o_ref,
               kb_ref, vb_ref, acc_ref, carry_ref, live_ref, *, tile, n_sub):
    assert n_sub >= 2
    n_pairs = SB_HEADS // 2
    pairs = range(n_pairs)
    i = pl.program_id(1)

    @pl.when(i == 0)
    def _():
        low = lax.broadcasted_iota(jnp.int32, (1, SB_WIDTH), 1) % LANES < SB_HEAD_DIM
        for src, dst in ((k_ref, kb_ref), (v_ref, vb_ref)):
            dst[0] = jnp.where(low, src[...], 0)
            dst[1] = jnp.where(low, 0, src[...])

    def both_heads(ref, keys, p):
        cols = slice(p * LANES, (p + 1) * LANES)
        return jnp.concatenate([ref[0, keys, cols], ref[1, keys, cols]], axis=0)

    def window(jobs, diagonal_first):
        units = [(r, p, b) for r, key_starts in jobs for p in pairs for b in range(len(key_starts))]
        keys = {r: [pl.ds(ks, tile) for ks in key_starts] for r, key_starts in jobs}
        zn, logsig, totals, after = {}, {}, {}, {}
        state = {}

        def scores(u):
            r, p, b = u
            zn[u] = _dot_nt(q_ref[r * tile:(r + 1) * tile, p * LANES:(p + 1) * LANES],
                            both_heads(kb_ref, keys[r][b], p))

        def log_terms(u):
            r, p, b = u
            z = zn.pop(u)
            if diagonal_first and b == 0:
                z = jnp.maximum(z, not_causal_ref[...])
            log1m = jnp.minimum(z, 0.0) - jnp.log2(1.0 + jnp.exp2(-jnp.abs(z)))
            logsig[u] = log1m - z
            hi, lo = _split_bf16(log1m)
            totals[u] = (log1m[:, 0:1], log1m[:, tile:tile + 1])
            after[u] = _dot(jnp.concatenate([hi, lo], axis=1), later2_ref[...])

        def weights_and_values(u):
            r, p, b = u
            if b == 0:
                state["carry"] = None if diagonal_first else carry_ref[r, p]
                state["pv"] = None
                if p == 0:
                    state["live"] = None
            carry = state["carry"]
            aft = after.pop(u)
            first = totals.pop(u)
            total = jnp.concatenate(
                [jnp.broadcast_to(aft[:, 0:1] + first[0], (tile, tile)),
                 jnp.broadcast_to(aft[:, tile:tile + 1] + first[1], (tile, tile))], axis=1)
            if carry is not None:
                aft = aft + carry
                total = total + carry
            state["carry"] = total
            w = jnp.exp2(logsig.pop(u) + aft)
            pv = _dot(w.astype(BF16), both_heads(vb_ref, keys[r][b], p))
            state["pv"] = pv if state["pv"] is None else state["pv"] + pv
            if b == len(keys[r]) - 1:
                carry_ref[r, p] = total
                state["live"] = total if state["live"] is None else jnp.maximum(state["live"], total)
                if diagonal_first:
                    acc_ref[r, p] = state["pv"]
                else:
                    acc_ref[r, p] += state["pv"]
                if p == n_pairs - 1:
                    live_ref[r] = jnp.max(state["live"])

        lag = 2
        n = len(units)
        for t in range(n + 2 * lag):
            if t < n:
                scores(units[t])
            if 0 <= t - lag < n:
                log_terms(units[t - lag])
            if 0 <= t - 2 * lag < n:
                weights_and_values(units[t - 2 * lag])

    def first_blocks(r, n_blocks):
        g0 = (i * n_sub + r) * tile
        return (r, [g0 - b * tile for b in range(n_blocks)])

    @pl.when(i == 0)
    def _():
        window([first_blocks(r, min(r + 1, 3)) for r in range(n_sub)], True)

    @pl.when(i > 0)
    def _():
        window([first_blocks(r, 3) for r in range(n_sub)], True)

    for r in range(n_sub):
        def cond(j, r=r):
            return jnp.logical_and(j >= 0, live_ref[r] > SB_DEAD_LOG2_WEIGHT)

        def body(j, r=r):
            window([(r, [j * tile])], False)
            return j - 1

        lax.while_loop(cond, body, i * n_sub + r - 3)

    outs = [[acc_ref[r, p] for p in pairs] for r in range(n_sub)]
    sumsq = [[_dot(jnp.concatenate(_split_bf16(o * o), axis=1), head_ones_ref[...]) for o in row] for row in outs]
    for r in range(n_sub):
        for p in pairs:
            inv = lax.rsqrt(sumsq[r][p] * (1.0 / SB_HEAD_DIM) + EPS)
            o_ref[r * tile:(r + 1) * tile, p * LANES:(p + 1) * LANES] = (
                outs[r][p] * inv * g_ref[:, p * LANES:(p + 1) * LANES]).astype(BF16)


def _sb_attention(p_act, g, *, batch, seq, tile, n_sub):
    T = batch * seq
    tq = n_sub * tile
    nq = seq // tq
    n_pairs = SB_HEADS // 2
    later2, not_causal, head_ones = _sb_constants(tile)
    return pl.pallas_call(
        functools.partial(_sb_kernel, tile=tile, n_sub=n_sub),
        grid=(batch, nq),
        in_specs=[
            pl.BlockSpec((tq, SB_WIDTH), lambda b, i: (b * nq + i, P_SB_Q // SB_WIDTH)),
            pl.BlockSpec((seq, SB_WIDTH), lambda b, i: (b, P_SB_K // SB_WIDTH)),
            pl.BlockSpec((seq, SB_WIDTH), lambda b, i: (b, P_SB_V // SB_WIDTH)),
            pl.BlockSpec((1, SB_WIDTH), lambda b, i: (0, 0)),
            pl.BlockSpec(later2.shape, lambda b, i: (0, 0)),
            pl.BlockSpec(not_causal.shape, lambda b, i: (0, 0)),
            pl.BlockSpec(head_ones.shape, lambda b, i: (0, 0)),
        ],
        out_specs=pl.BlockSpec((tq, SB_WIDTH), lambda b, i: (b * nq + i, 0)),
        out_shape=jax.ShapeDtypeStruct((T, SB_WIDTH), BF16),
        scratch_shapes=[
            pltpu.VMEM((2, seq, SB_WIDTH), BF16),
            pltpu.VMEM((2, seq, SB_WIDTH), BF16),
            pltpu.VMEM((n_sub, n_pairs, tile, LANES), F32),
            pltpu.VMEM((n_sub, n_pairs, tile, 2 * tile), F32),
            pltpu.SMEM((n_sub,), F32),
        ],
        compiler_params=pltpu.CompilerParams(
            dimension_semantics=("arbitrary", "arbitrary"), vmem_limit_bytes=VMEM_LIMIT_BYTES),
        name="sb",
    )(p_act, p_act, p_act, g, later2, not_causal, head_ones)


def _gla_kernel(q_ref, k_ref, v_ref, og_ref, la_ref, g_ref, o_ref, state_ref, *, seq, step_rows):
    C = GLA_CHUNK
    R = step_rows
    n_chunks = R // C
    pairs = range(GLA_HEADS // 2)
    low_half = lax.broadcasted_iota(jnp.int32, (1, LANES), 1) < GLA_DK
    ri = lax.broadcasted_iota(jnp.int32, (R, R), 0)
    ci = lax.broadcasted_iota(jnp.int32, (R, R), 1)
    same_chunk = (ri // C) == (ci // C)
    causal = same_chunk & (ci <= ri)
    prefix = jnp.where(causal, 1.0, 0.0).astype(BF16)
    whole = jnp.where(same_chunk, 1.0, 0.0).astype(BF16)
    sums_lhs = jnp.concatenate([jnp.concatenate([prefix, prefix], axis=1),
                                jnp.concatenate([whole, whole], axis=1)], axis=0)
    state_ref[...] = jnp.zeros_like(state_ref)

    def step(s, carry):
        rows = pl.ds(pl.multiple_of(s * R, R), R)
        la_hi, la_lo = _split_bf16(la_ref[rows, :])
        sums = _dot(sums_lhs, jnp.concatenate([la_hi, la_lo], axis=0))
        b, b_end = sums[:R], sums[R:]
        q = q_ref[rows, :].astype(F32) * (GLA_DK ** -0.5)
        k = k_ref[rows, :].astype(F32)
        q_dec = q * jnp.exp(b)
        k_inv = (k * jnp.exp(-b)).astype(BF16)
        k_end = k * jnp.exp(b_end - b)
        decay = jnp.exp(b_end)
        q_half, k_end_half = [], []
        for p in pairs:
            ks = slice(p * LANES, (p + 1) * LANES)
            q_half.append((jnp.where(low_half, q_dec[:, ks], 0.0).astype(BF16),
                           jnp.where(low_half, 0.0, q_dec[:, ks]).astype(BF16)))
            k_end_half.append((jnp.where(low_half, k_end[:, ks], 0.0).astype(BF16),
                               jnp.where(low_half, 0.0, k_end[:, ks]).astype(BF16)))
        v = [v_ref[rows, h * GLA_DV:(h + 1) * GLA_DV] for h in range(GLA_HEADS)]
        attn = [jnp.where(causal, _dot_nt(q_half[h // 2][h % 2], k_inv[:, (h // 2) * LANES:(h // 2 + 1) * LANES]),
                          0.0).astype(BF16) for h in range(GLA_HEADS)]
        o_intra = [_dot(attn[h], v[h]) for h in range(GLA_HEADS)]
        kv = [[_dot_tn(jnp.concatenate([v[2 * p][c * C:(c + 1) * C], v[2 * p + 1][c * C:(c + 1) * C]], axis=0),
                       jnp.concatenate([k_end_half[p][0][c * C:(c + 1) * C],
                                        k_end_half[p][1][c * C:(c + 1) * C]], axis=0))
               for c in range(n_chunks)] for p in pairs]
        o_inter = [[] for _ in range(GLA_HEADS)]
        for p in pairs:
            state_t = state_ref[p]
            for c in range(n_chunks):
                state_b = state_t.astype(BF16)
                for hh in range(2):
                    o_inter[2 * p + hh].append(_dot_nt(q_half[p][hh][c * C:(c + 1) * C], state_b))
                state_t = state_t * decay[c * C:c * C + 1, p * LANES:(p + 1) * LANES] + kv[p][c]
            state_ref[p] = state_t
        for h in range(GLA_HEADS):
            vs = slice(h * GLA_DV, (h + 1) * GLA_DV)
            o = o_intra[h] + jnp.concatenate(o_inter[h], axis=0)
            inv = lax.rsqrt(jnp.mean(o * o, axis=-1, keepdims=True) + EPS)
            gate = og_ref[rows, vs].astype(F32)
            gate = gate / (1.0 + jnp.exp(-gate))
            o_ref[rows, vs] = (o * inv * g_ref[:, vs] * gate).astype(BF16)
        return carry

    lax.fori_loop(0, seq // R, step, 0, unroll=2)


def _gla(p_act, log_a, g, *, batch, seq):
    T = batch * seq
    n_pairs = GLA_HEADS // 2
    return pl.pallas_call(
        functools.partial(_gla_kernel, seq=seq, step_rows=min(256, seq)),
        grid=(batch,),
        in_specs=[
            pl.BlockSpec((seq, GLA_KEY_WIDTH), lambda b: (b, P_G_Q // GLA_KEY_WIDTH)),
            pl.BlockSpec((seq, GLA_KEY_WIDTH), lambda b: (b, P_G_K // GLA_KEY_WIDTH)),
            pl.BlockSpec((seq, GLA_WIDTH), lambda b: (b, P_G_V // GLA_WIDTH)),
            pl.BlockSpec((seq, GLA_WIDTH), lambda b: (b, P_G_OG // GLA_WIDTH)),
            pl.BlockSpec((seq, GLA_KEY_WIDTH), lambda b: (b, 0)),
            pl.BlockSpec((1, GLA_WIDTH), lambda b: (0, 0)),
        ],
        out_specs=pl.BlockSpec((seq, GLA_WIDTH), lambda b: (b, 0)),
        out_shape=jax.ShapeDtypeStruct((T, GLA_WIDTH), BF16),
        scratch_shapes=[pltpu.VMEM((n_pairs, GLA_DV, LANES), F32)],
        compiler_params=pltpu.CompilerParams(
            dimension_semantics=("arbitrary",), vmem_limit_bytes=VMEM_LIMIT_BYTES),
        name="gla",
    )(p_act, p_act, p_act, p_act, log_a, g)


def _rms(x, g):
    ms = jnp.mean(x * x, axis=-1, keepdims=True)
    return x * lax.rsqrt(ms + EPS) * g


def _ffn_kernel(x_ref, osb_ref, ogla_ref, wo_sb_ref, wo_gla_ref, g2_ref, wup_ref, cw_ref, cb_ref,
                wdn_ref, gf_ref, out_ref, x1_ref, h2_ref, acc_ref, u00_ref, u01_ref, u10_ref, u11_ref,
                halo_ref,
                *, tm, fc, tiles_per_seq):
    t = pl.program_id(0)
    n_chunks = D_FF // fc
    H = SUBLANES

    @pl.when(t % tiles_per_seq == 0)
    def _():
        halo_ref[...] = jnp.zeros_like(halo_ref)

    x1 = x_ref[...] + _dot(osb_ref[...], wo_sb_ref[...]) + _dot(ogla_ref[...], wo_gla_ref[...])
    x1_ref[...] = x1
    h2_ref[...] = _rms(x1, g2_ref[...]).astype(BF16)

    u_bufs = ((u00_ref, u01_ref), (u10_ref, u11_ref))

    def cols_of(part, c):
        return slice(part * D_FF + c * fc, part * D_FF + (c + 1) * fc)

    def up_project(c, part):
        buf = u_bufs[c % 2][part]
        u = _dot(h2_ref[...], wup_ref[:, cols_of(part, c)])
        buf[0:H, :] = halo_ref[part, c]
        buf[H:H + tm, :] = u
        halo_ref[part, c] = u[tm - H:tm, :]

    def conv_gate_down(c, r0, nrows):
        conv = []
        for part in range(2):
            buf = u_bufs[c % 2][part]
            cw = cw_ref[:, cols_of(part, c)]
            conv.append(cb_ref[:, cols_of(part, c)] + cw[2:3, :] * buf[H + r0:H + r0 + nrows, :]
                        + cw[1:2, :] * buf[H - 1 + r0:H - 1 + r0 + nrows, :]
                        + cw[0:1, :] * buf[H - 2 + r0:H - 2 + r0 + nrows, :])
        a, val = conv
        gated = (a / (1.0 + jnp.exp(-a)) * val).astype(BF16)
        down = _dot(gated, wdn_ref[c * fc:(c + 1) * fc, :])
        if c == 0:
            acc_ref[r0:r0 + nrows, :] = down
        else:
            acc_ref[r0:r0 + nrows, :] += down

    half = tm // 2
    up_project(0, 0)
    up_project(0, 1)
    for c in range(n_chunks):
        for part in range(2):
            if c + 1 < n_chunks:
                up_project(c + 1, part)
            conv_gate_down(c, part * half, half)

    out_ref[...] = _rms(x1_ref[...] + acc_ref[...], gf_ref[...])


def _ffn(x2d, o_sb, o_gla, wo_sb, wo_gla, g2, w_up, conv_w, conv_b, w_down, gf, *, tm, fc, seq):
    T = x2d.shape[0]
    n_chunks = D_FF // fc
    const = lambda i: (0, 0)
    resident = functools.partial(pl.BlockSpec, index_map=const, pipeline_mode=pl.Buffered(1))
    return pl.pallas_call(
        functools.partial(_ffn_kernel, tm=tm, fc=fc, tiles_per_seq=seq // tm),
        grid=(T // tm,),
        in_specs=[
            pl.BlockSpec((tm, D_MODEL), lambda i: (i, 0)),
            pl.BlockSpec((tm, SB_WIDTH), lambda i: (i, 0)),
            pl.BlockSpec((tm, GLA_WIDTH), lambda i: (i, 0)),
            resident((SB_WIDTH, D_MODEL)),
            resident((GLA_WIDTH, D_MODEL)),
            resident((1, D_MODEL)),
            resident((D_MODEL, 2 * D_FF)),
            resident((CONV_WIDTH, 2 * D_FF)),
            resident((1, 2 * D_FF)),
            resident((D_FF, D_MODEL)),
            resident((1, D_MODEL)),
        ],
        out_specs=pl.BlockSpec((tm, D_MODEL), lambda i: (i, 0)),
        out_shape=jax.ShapeDtypeStruct((T, D_MODEL), F32),
        scratch_shapes=[
            pltpu.VMEM((tm, D_MODEL), F32),
            pltpu.VMEM((tm, D_MODEL), BF16),
            pltpu.VMEM((tm, D_MODEL), F32),
            pltpu.VMEM((SUBLANES + tm, fc), F32),
            pltpu.VMEM((SUBLANES + tm, fc), F32),
            pltpu.VMEM((SUBLANES + tm, fc), F32),
            pltpu.VMEM((SUBLANES + tm, fc), F32),
            pltpu.VMEM((2, n_chunks, SUBLANES, fc), F32),
        ],
        compiler_params=pltpu.CompilerParams(
            dimension_semantics=("arbitrary",), vmem_limit_bytes=VMEM_LIMIT_BYTES),
        name="ffn",
    )(x2d, o_sb, o_gla, wo_sb, wo_gla, g2, w_up, conv_w, conv_b, w_down, gf)


def kernel(x, attn_norm_g, w_in, w_gate_up, b_gate_up, sb_out_g, gla_out_g, w_out, ffn_norm_g,
           w_ffn_up, conv_w, conv_b, w_ffn_down, final_norm_g):
    batch, seq, d_model = x.shape
    assert d_model == D_MODEL and attn_norm_g.shape[0] == 1
    T = batch * seq
    x2d = x.reshape(T, D_MODEL)

    lr0 = P_G_OG
    w_in0 = w_in[0]
    sb_q_scale = -(SB_HEAD_DIM ** -0.5) * LOG2_E
    w_main = jnp.concatenate([w_in0[:, :SB_WIDTH] * sb_q_scale, w_in0[:, SB_WIDTH:lr0],
                              w_in0[:, lr0 + GLA_GATE_RANK:]], axis=1).astype(BF16)
    w_lr = jnp.pad(w_in0[:, lr0:lr0 + GLA_GATE_RANK], ((0, 0), (0, LANES - GLA_GATE_RANK))).astype(BF16)
    w_gate = jnp.pad(w_gate_up[0], ((0, LANES - GLA_GATE_RANK), (0, 0))).astype(BF16)

    tm = min(512, seq)
    p_act, log_a = _inproj(x2d, attn_norm_g, w_main, w_lr, w_gate, b_gate_up, tm=min(1024, seq))
    o_sb = _sb_attention(p_act, sb_out_g, batch=batch, seq=seq, tile=LANES, n_sub=4)
    o_gla = _gla(p_act, log_a, gla_out_g, batch=batch, seq=seq)
    out = _ffn(x2d, o_sb, o_gla,
               w_out[0, :SB_WIDTH].astype(BF16), w_out[0, SB_WIDTH:].astype(BF16),
               ffn_norm_g, w_ffn_up[0].astype(BF16), conv_w[0], conv_b, w_ffn_down[0].astype(BF16),
               final_norm_g.reshape(1, D_MODEL), tm=tm, fc=256, seq=seq)
    return out.reshape(batch, seq, D_MODEL)
```

```python
import functools

import jax
import jax.numpy as jnp
from jax import lax
from jax.experimental import pallas as pl
from jax.experimental.pallas import tpu as pltpu

F32 = jnp.float32
BF16 = jnp.bfloat16

D_MODEL = 1024
SB_HEAD_DIM = 64
SB_WIDTH = 512
SB_HEADS = SB_WIDTH // SB_HEAD_DIM
GLA_HEADS = 4
GLA_WIDTH = 512
GLA_DV = GLA_WIDTH // GLA_HEADS
GLA_DK = GLA_DV // 2
GLA_KEY_WIDTH = GLA_HEADS * GLA_DK
GLA_GATE_RANK = 16
GLA_GATE_NORMALIZER = 16.0
GLA_CHUNK = 64
D_FF = 2816
CONV_WIDTH = 3
EPS = 1e-6

LANES = 128
SUBLANES = 8
VMEM_LIMIT_BYTES = 56 * 1024 * 1024

P_SB_Q, P_SB_K, P_SB_V = 0, 512, 1024
P_G_Q, P_G_K, P_G_V, P_G_OG = 1536, 1792, 2048, 2560
P_COLS = 3072

LOG2_E = 1.4426950408889634
SB_DEAD_LOG2_WEIGHT = -127.0
SB_MASKED_SCORE = 1.0e4


def _dot(a, b):
    return jnp.dot(a, b, preferred_element_type=F32)


def _dot_nt(a, b):
    return lax.dot_general(a, b, (((1,), (1,)), ((), ())), preferred_element_type=F32)


def _dot_tn(a, b):
    return lax.dot_general(a, b, (((0,), (0,)), ((), ())), preferred_element_type=F32)


def _split_bf16(x):
    hi = x.astype(BF16)
    lo = (x - hi.astype(F32)).astype(BF16)
    return hi, lo


def _log_sigmoid(x):
    return jnp.minimum(x, 0.0) - jnp.log(1.0 + jnp.exp(-jnp.abs(x)))


def _inproj_kernel(x_ref, g_ref, w_ref, wlr_ref, wg_ref, bg_ref, p_ref, la_ref, *, col_chunk, row_parts):
    part = x_ref.shape[0] // row_parts
    rows = [slice(r * part, (r + 1) * part) for r in range(row_parts)]
    y = {}
    for c in range(0, P_COLS, col_chunk):
        for r in range(row_parts):
            if r not in y:
                x = x_ref[rows[r], :]
                ms = jnp.mean(x * x, axis=-1, keepdims=True)
                y[r] = (x * lax.rsqrt(ms + EPS) * g_ref[...]).astype(BF16)
            p_ref[rows[r], c:c + col_chunk] = _dot(y[r], w_ref[:, c:c + col_chunk]).astype(BF16)
            if c == col_chunk:
                g_lr = _dot(y[r], wlr_ref[...])
                logit = _dot(g_lr.astype(BF16), wg_ref[...]) + bg_ref[...]
                la_ref[rows[r], :] = _log_sigmoid(logit) * (1.0 / GLA_GATE_NORMALIZER)


def _inproj(x2d, g, w_main, w_lr, w_gate, b_gate, *, tm):
    T = x2d.shape[0]
    const = lambda i: (0, 0)
    return pl.pallas_call(
        functools.partial(_inproj_kernel, col_chunk=512, row_parts=1),
        grid=(T // tm,),
        in_specs=[
            pl.BlockSpec((tm, D_MODEL), lambda i: (i, 0)),
            pl.BlockSpec((1, D_MODEL), const),
            pl.BlockSpec((D_MODEL, P_COLS), const),
            pl.BlockSpec((D_MODEL, LANES), const),
            pl.BlockSpec((LANES, GLA_KEY_WIDTH), const),
            pl.BlockSpec((1, GLA_KEY_WIDTH), const),
        ],
        out_specs=[
            pl.BlockSpec((tm, P_COLS), lambda i: (i, 0)),
            pl.BlockSpec((tm, GLA_KEY_WIDTH), lambda i: (i, 0)),
        ],
        out_shape=[
            jax.ShapeDtypeStruct((T, P_COLS), BF16),
            jax.ShapeDtypeStruct((T, GLA_KEY_WIDTH), F32),
        ],
        compiler_params=pltpu.CompilerParams(
            dimension_semantics=("arbitrary",), vmem_limit_bytes=VMEM_LIMIT_BYTES),
        name="inproj",
    )(x2d, g, w_main, w_lr, w_gate, b_gate)


def _sb_constants(tile):
    row = jnp.arange(4 * tile)[:, None] % (2 * tile)
    col = jnp.arange(2 * tile)[None, :]
    later2 = ((row > col) & ((row >= tile) == (col >= tile))).astype(BF16)
    t_idx = jnp.arange(tile)[:, None]
    s_idx = jnp.arange(2 * tile)[None, :] % tile
    not_causal = jnp.where(s_idx < t_idx, -SB_MASKED_SCORE * 1e30, SB_MASKED_SCORE).astype(F32)
    r = jnp.arange(2 * LANES)[:, None] % LANES
    c = jnp.arange(LANES)[None, :]
    head_ones = ((r // SB_HEAD_DIM) == (c // SB_HEAD_DIM)).astype(BF16)
    return later2, not_causal, head_ones


def _sb_kernel(q_ref, k_ref, v_ref, g_ref, later2_ref, not_causal_ref, head_ones_ref, o_ref,
               kb_ref, vb_ref, acc_ref, carry_ref, live_ref, *, tile, n_sub):
    assert n_sub >= 2
    n_pairs = SB_HEADS // 2
    pairs = range(n_pairs)
    i = pl.program_id(1)

    @pl.when(i == 0)
    def _():
        low = lax.broadcasted_iota(jnp.int32, (1, SB_WIDTH), 1) % LANES < SB_HEAD_DIM
        for src, dst in ((k_ref, kb_ref), (v_ref, vb_ref)):
            dst[0] = jnp.where(low, src[...], 0)
            dst[1] = jnp.where(low, 0, src[...])

    def both_heads(ref, keys, p):
        cols = slice(p * LANES, (p + 1) * LANES)
        return jnp.concatenate([ref[0, keys, cols], ref[1, keys, cols]], axis=0)

    def window(jobs, diagonal_first):
        units = [(r, p, b) for r, key_starts in jobs for p in pairs for b in range(len(key_starts))]
        keys = {r: [pl.ds(ks, tile) for ks in key_starts] for r, key_starts in jobs}
        zn, logsig, totals, after = {}, {}, {}, {}
        state = {}

        def scores(u):
            r, p, b = u
            zn[u] = _dot_nt(q_ref[r * tile:(r + 1) * tile, p * LANES:(p + 1) * LANES],
                            both_heads(kb_ref, keys[r][b], p))

        def log_terms(u):
            r, p, b = u
            z = zn.pop(u)
            if diagonal_first and b == 0:
                z = jnp.maximum(z, not_causal_ref[...])
            log1m = jnp.minimum(z, 0.0) - jnp.log2(1.0 + jnp.exp2(-jnp.abs(z)))
            logsig[u] = log1m - z
            hi, lo = _split_bf16(log1m)
            totals[u] = (log1m[:, 0:1], log1m[:, tile:tile + 1])
            after[u] = _dot(jnp.concatenate([hi, lo], axis=1), later2_ref[...])

        def weights_and_values(u):
            r, p, b = u
            if b == 0:
                state["carry"] = None if diagonal_first else carry_ref[r, p]
                state["pv"] = None
                if p == 0:
                    state["live"] = None
            carry = state["carry"]
            aft = after.pop(u)
            first = totals.pop(u)
            total = jnp.concatenate(
                [jnp.broadcast_to(aft[:, 0:1] + first[0], (tile, tile)),
                 jnp.broadcast_to(aft[:, tile:tile + 1] + first[1], (tile, tile))], axis=1)
            if carry is not None:
                aft = aft + carry
                total = total + carry
            state["carry"] = total
            w = jnp.exp2(logsig.pop(u) + aft)
            pv = _dot(w.astype(BF16), both_heads(vb_ref, keys[r][b], p))
            state["pv"] = pv if state["pv"] is None else state["pv"] + pv
            if b == len(keys[r]) - 1:
                carry_ref[r, p] = total
                state["live"] = total if state["live"] is None else jnp.maximum(state["live"], total)
                if diagonal_first:
                    acc_ref[r, p] = state["pv"]
                else:
                    acc_ref[r, p] += state["pv"]
                if p == n_pairs - 1:
                    live_ref[r] = jnp.max(state["live"])

        lag = 2
        n = len(units)
        for t in range(n + 2 * lag):
            if t < n:
                scores(units[t])
            if 0 <= t - lag < n:
                log_terms(units[t - lag])
            if 0 <= t - 2 * lag < n:
                weights_and_values(units[t - 2 * lag])

    def first_blocks(r, n_blocks):
        g0 = (i * n_sub + r) * tile
        return (r, [g0 - b * tile for b in range(n_blocks)])

    @pl.when(i == 0)
    def _():
        window([first_blocks(r, min(r + 1, 3)) for r in range(n_sub)], True)

    @pl.when(i > 0)
    def _():
        window([first_blocks(r, 3) for r in range(n_sub)], True)

    for r in range(n_sub):
        def cond(j, r=r):
            return jnp.logical_and(j >= 0, live_ref[r] > SB_DEAD_LOG2_WEIGHT)

        def body(j, r=r):
            window([(r, [j * tile])], False)
            return j - 1

        lax.while_loop(cond, body, i * n_sub + r - 3)

    outs = [[acc_ref[r, p] for p in pairs] for r in range(n_sub)]
    sumsq = [[_dot(jnp.concatenate(_split_bf16(o * o), axis=1), head_ones_ref[...]) for o in row] for row in outs]
    for r in range(n_sub):
        for p in pairs:
            inv = lax.rsqrt(sumsq[r][p] * (1.0 / SB_HEAD_DIM) + EPS)
            o_ref[r * tile:(r + 1) * tile, p * LANES:(p + 1) * LANES] = (
                outs[r][p] * inv * g_ref[:, p * LANES:(p + 1) * LANES]).astype(BF16)


def _sb_attention(p_act, g, *, batch, seq, tile, n_sub):
    T = batch * seq
    tq = n_sub * tile
    nq = seq // tq
    n_pairs = SB_HEADS // 2
    later2, not_causal, head_ones = _sb_constants(tile)
    return pl.pallas_call(
        functools.partial(_sb_kernel, tile=tile, n_sub=n_sub),
        grid=(batch, nq),
        in_specs=[
            pl.BlockSpec((tq, SB_WIDTH), lambda b, i: (b * nq + i, P_SB_Q // SB_WIDTH)),
            pl.BlockSpec((seq, SB_WIDTH), lambda b, i: (b, P_SB_K // SB_WIDTH)),
            pl.BlockSpec((seq, SB_WIDTH), lambda b, i: (b, P_SB_V // SB_WIDTH)),
            pl.BlockSpec((1, SB_WIDTH), lambda b, i: (0, 0)),
            pl.BlockSpec(later2.shape, lambda b, i: (0, 0)),
            pl.BlockSpec(not_causal.shape, lambda b, i: (0, 0)),
            pl.BlockSpec(head_ones.shape, lambda b, i: (0, 0)),
        ],
        out_specs=pl.BlockSpec((tq, SB_WIDTH), lambda b, i: (b * nq + i, 0)),
        out_shape=jax.ShapeDtypeStruct((T, SB_WIDTH), BF16),
        scratch_shapes=[
            pltpu.VMEM((2, seq, SB_WIDTH), BF16),
            pltpu.VMEM((2, seq, SB_WIDTH), BF16),
            pltpu.VMEM((n_sub, n_pairs, tile, LANES), F32),
            pltpu.VMEM((n_sub, n_pairs, tile, 2 * tile), F32),
            pltpu.SMEM((n_sub,), F32),
        ],
        compiler_params=pltpu.CompilerParams(
            dimension_semantics=("arbitrary", "arbitrary"), vmem_limit_bytes=VMEM_LIMIT_BYTES),
        name="sb",
    )(p_act, p_act, p_act, g, later2, not_causal, head_ones)


def _gla_kernel(q_ref, k_ref, v_ref, og_ref, la_ref, g_ref, o_ref, state_ref, *, seq, step_rows):
    C = GLA_CHUNK
    R = step_rows
    n_chunks = R // C
    pairs = range(GLA_HEADS // 2)
    low_half = lax.broadcasted_iota(jnp.int32, (1, LANES), 1) < GLA_DK
    ri = lax.broadcasted_iota(jnp.int32, (R, R), 0)
    ci = lax.broadcasted_iota(jnp.int32, (R, R), 1)
    same_chunk = (ri // C) == (ci // C)
    causal = same_chunk & (ci <= ri)
    prefix = jnp.where(causal, 1.0, 0.0).astype(BF16)
    whole = jnp.where(same_chunk, 1.0, 0.0).astype(BF16)
    sums_lhs = jnp.concatenate([jnp.concatenate([prefix, prefix], axis=1),
                                jnp.concatenate([whole, whole], axis=1)], axis=0)
    state_ref[...] = jnp.zeros_like(state_ref)

    def front(s):
        rows = pl.ds(pl.multiple_of(s * R, R), R)
        la_hi, la_lo = _split_bf16(la_ref[rows, :])
        sums = _dot(sums_lhs, jnp.concatenate([la_hi, la_lo], axis=0))
        b, b_end = sums[:R], sums[R:]
        q = q_ref[rows, :].astype(F32) * (GLA_DK ** -0.5)
        k = k_ref[rows, :].astype(F32)
        q_dec = q * jnp.exp(b)
        k_inv = (k * jnp.exp(-b)).astype(BF16)
        k_end = k * jnp.exp(b_end - b)
        decay = jnp.exp(b_end)
        q_half, k_end_half = [], []
        for p in pairs:
            ks = slice(p * LANES, (p + 1) * LANES)
            q_half.append((jnp.where(low_half, q_dec[:, ks], 0.0).astype(BF16),
                           jnp.where(low_half, 0.0, q_dec[:, ks]).astype(BF16)))
            k_end_half.append((jnp.where(low_half, k_end[:, ks], 0.0).astype(BF16),
                               jnp.where(low_half, 0.0, k_end[:, ks]).astype(BF16)))
        v = [v_ref[rows, h * GLA_DV:(h + 1) * GLA_DV] for h in range(GLA_HEADS)]
        attn = [jnp.where(causal, _dot_nt(q_half[h // 2][h % 2], k_inv[:, (h // 2) * LANES:(h // 2 + 1) * LANES]),
                          0.0).astype(BF16) for h in range(GLA_HEADS)]
        o_intra = [_dot(attn[h], v[h]) for h in range(GLA_HEADS)]
        kv = [[_dot_tn(jnp.concatenate([v[2 * p][c * C:(c + 1) * C], v[2 * p + 1][c * C:(c + 1) * C]], axis=0),
                       jnp.concatenate([k_end_half[p][0][c * C:(c + 1) * C],
                                        k_end_half[p][1][c * C:(c + 1) * C]], axis=0))
               for c in range(n_chunks)] for p in pairs]
        return rows, q_half, decay, o_intra, kv

    def back(rows, q_half, decay, o_intra, kv):
        o_inter = [[] for _ in range(GLA_HEADS)]
        for p in pairs:
            state_t = state_ref[p]
            for c in range(n_chunks):
                state_b = state_t.astype(BF16)
                for hh in range(2):
                    o_inter[2 * p + hh].append(_dot_nt(q_half[p][hh][c * C:(c + 1) * C], state_b))
                state_t = state_t * decay[c * C:c * C + 1, p * LANES:(p + 1) * LANES] + kv[p][c]
            state_ref[p] = state_t
        for h in range(GLA_HEADS):
            vs = slice(h * GLA_DV, (h + 1) * GLA_DV)
            o = o_intra[h] + jnp.concatenate(o_inter[h], axis=0)
            inv = lax.rsqrt(jnp.mean(o * o, axis=-1, keepdims=True) + EPS)
            gate = og_ref[rows, vs].astype(F32)
            gate = gate / (1.0 + jnp.exp(-gate))
            o_ref[rows, vs] = (o * inv * g_ref[:, vs] * gate).astype(BF16)

    def group(gi, carry):
        s0 = gi * steps_per_group
        pending = front(s0)
        for d in range(1, steps_per_group):
            ahead = front(s0 + d)
            back(*pending)
            pending = ahead
        back(*pending)
        return carry

    n_steps = seq // R
    steps_per_group = 8 if n_steps % 8 == 0 else 1
    lax.fori_loop(0, n_steps // steps_per_group, group, 0)


def _gla(p_act, log_a, g, *, batch, seq):
    T = batch * seq
    n_pairs = GLA_HEADS // 2
    return pl.pallas_call(
        functools.partial(_gla_kernel, seq=seq, step_rows=min(256, seq)),
        grid=(batch,),
        in_specs=[
            pl.BlockSpec((seq, GLA_KEY_WIDTH), lambda b: (b, P_G_Q // GLA_KEY_WIDTH)),
            pl.BlockSpec((seq, GLA_KEY_WIDTH), lambda b: (b, P_G_K // GLA_KEY_WIDTH)),
            pl.BlockSpec((seq, GLA_WIDTH), lambda b: (b, P_G_V // GLA_WIDTH)),
            pl.BlockSpec((seq, GLA_WIDTH), lambda b: (b, P_G_OG // GLA_WIDTH)),
            pl.BlockSpec((seq, GLA_KEY_WIDTH), lambda b: (b, 0)),
            pl.BlockSpec((1, GLA_WIDTH), lambda b: (0, 0)),
        ],
        out_specs=pl.BlockSpec((seq, GLA_WIDTH), lambda b: (b, 0)),
        out_shape=jax.ShapeDtypeStruct((T, GLA_WIDTH), BF16),
        scratch_shapes=[pltpu.VMEM((n_pairs, GLA_DV, LANES), F32)],
        compiler_params=pltpu.CompilerParams(
            dimension_semantics=("arbitrary",), vmem_limit_bytes=VMEM_LIMIT_BYTES),
        name="gla",
    )(p_act, p_act, p_act, p_act, log_a, g)


def _rms(x, g):
    ms = jnp.mean(x * x, axis=-1, keepdims=True)
    return x * lax.rsqrt(ms + EPS) * g


def _ffn_kernel(x_ref, osb_ref, ogla_ref, wo_sb_ref, wo_gla_ref, g2_ref, wup_ref, cw_ref, cb_ref,
                wdn_ref, gf_ref, out_ref, x1_ref, h2_ref, acc_ref, u00_ref, u01_ref, u10_ref, u11_ref,
                halo_ref,
                *, tm, fc, tiles_per_seq):
    t = pl.program_id(0)
    n_chunks = D_FF // fc
    H = SUBLANES

    @pl.when(t % tiles_per_seq == 0)
    def _():
        halo_ref[...] = jnp.zeros_like(halo_ref)

    x1 = x_ref[...] + _dot(osb_ref[...], wo_sb_ref[...]) + _dot(ogla_ref[...], wo_gla_ref[...])
    x1_ref[...] = x1
    h2_ref[...] = _rms(x1, g2_ref[...]).astype(BF16)

    u_bufs = ((u00_ref, u01_ref), (u10_ref, u11_ref))

    def cols_of(part, c):
        return slice(part * D_FF + c * fc, part * D_FF + (c + 1) * fc)

    def up_project(c, part):
        buf = u_bufs[c % 2][part]
        u = _dot(h2_ref[...], wup_ref[:, cols_of(part, c)])
        buf[0:H, :] = halo_ref[part, c]
        buf[H:H + tm, :] = u
        halo_ref[part, c] = u[tm - H:tm, :]

    def conv_gate_down(c, r0, nrows):
        conv = []
        for part in range(2):
            buf = u_bufs[c % 2][part]
            cw = cw_ref[:, cols_of(part, c)]
            win = buf[r0:H + r0 + nrows, :]
            conv.append(cb_ref[:, cols_of(part, c)] + cw[2:3, :] * win[H:, :]
                        + cw[1:2, :] * pltpu.roll(win, 1, axis=0)[H:, :]
                        + cw[0:1, :] * pltpu.roll(win, 2, axis=0)[H:, :])
        a, val = conv
        half_a = 0.5 * a
        gated = ((half_a + half_a * jnp.tanh(half_a)) * val).astype(BF16)
        down = _dot(gated, wdn_ref[c * fc:(c + 1) * fc, :])
        if c == 0:
            acc_ref[r0:r0 + nrows, :] = down
        else:
            acc_ref[r0:r0 + nrows, :] += down

    half = tm // 2
    up_project(0, 0)
    up_project(0, 1)
    for c in range(n_chunks):
        for part in range(2):
            if c + 1 < n_chunks:
                up_project(c + 1, part)
            conv_gate_down(c, part * half, half)

    out_ref[...] = _rms(x1_ref[...] + acc_ref[...], gf_ref[...])


def _ffn(x2d, o_sb, o_gla, wo_sb, wo_gla, g2, w_up, conv_w, conv_b, w_down, gf, *, tm, fc, seq):
    T = x2d.shape[0]
    n_chunks = D_FF // fc
    const = lambda i: (0, 0)
    resident = functools.partial(pl.BlockSpec, index_map=const, pipeline_mode=pl.Buffered(1))
    return pl.pallas_call(
        functools.partial(_ffn_kernel, tm=tm, fc=fc, tiles_per_seq=seq // tm),
        grid=(T // tm,),
        in_specs=[
            pl.BlockSpec((tm, D_MODEL), lambda i: (i, 0)),
            pl.BlockSpec((tm, SB_WIDTH), lambda i: (i, 0)),
            pl.BlockSpec((tm, GLA_WIDTH), lambda i: (i, 0)),
            resident((SB_WIDTH, D_MODEL)),
            resident((GLA_WIDTH, D_MODEL)),
            resident((1, D_MODEL)),
            resident((D_MODEL, 2 * D_FF)),
            resident((CONV_WIDTH, 2 * D_FF)),
            resident((1, 2 * D_FF)),
            resident((D_FF, D_MODEL)),
            resident((1, D_MODEL)),
        ],
        out_specs=pl.BlockSpec((tm, D_MODEL), lambda i: (i, 0)),
        out_shape=jax.ShapeDtypeStruct((T, D_MODEL), F32),
        scratch_shapes=[
            pltpu.VMEM((tm, D_MODEL), F32),
            pltpu.VMEM((tm, D_MODEL), BF16),
            pltpu.VMEM((tm, D_MODEL), F32),
            pltpu.VMEM((SUBLANES + tm, fc), F32),
            pltpu.VMEM((SUBLANES + tm, fc), F32),
            pltpu.VMEM((SUBLANES + tm, fc), F32),
            pltpu.VMEM((SUBLANES + tm, fc), F32),
            pltpu.VMEM((2, n_chunks, SUBLANES, fc), F32),
        ],
        compiler_params=pltpu.CompilerParams(
            dimension_semantics=("arbitrary",), vmem_limit_bytes=VMEM_LIMIT_BYTES),
        name="ffn",
    )(x2d, o_sb, o_gla, wo_sb, wo_gla, g2, w_up, conv_w, conv_b, w_down, gf)


def kernel(x, attn_norm_g, w_in, w_gate_up, b_gate_up, sb_out_g, gla_out_g, w_out, ffn_norm_g,
           w_ffn_up, conv_w, conv_b, w_ffn_down, final_norm_g):
    batch, seq, d_model = x.shape
    assert d_model == D_MODEL and attn_norm_g.shape[0] == 1
    T = batch * seq
    x2d = x.reshape(T, D_MODEL)

    lr0 = P_G_OG
    w_in0 = w_in[0]
    sb_q_scale = -(SB_HEAD_DIM ** -0.5) * LOG2_E
    w_main = jnp.concatenate([w_in0[:, :SB_WIDTH] * sb_q_scale, w_in0[:, SB_WIDTH:lr0],
                              w_in0[:, lr0 + GLA_GATE_RANK:]], axis=1).astype(BF16)
    w_lr = jnp.pad(w_in0[:, lr0:lr0 + GLA_GATE_RANK], ((0, 0), (0, LANES - GLA_GATE_RANK))).astype(BF16)
    w_gate = jnp.pad(w_gate_up[0], ((0, LANES - GLA_GATE_RANK), (0, 0))).astype(BF16)

    tm = min(512, seq)
    p_act, log_a = _inproj(x2d, attn_norm_g, w_main, w_lr, w_gate, b_gate_up, tm=tm)
    o_sb = _sb_attention(p_act, sb_out_g, batch=batch, seq=seq, tile=LANES, n_sub=4)
    o_gla = _gla(p_act, log_a, gla_out_g, batch=batch, seq=seq)
    out = _ffn(x2d, o_sb, o_gla,
               w_out[0, :SB_WIDTH].astype(BF16), w_out[0, SB_WIDTH:].astype(BF16),
               ffn_norm_g, w_ffn_up[0].astype(BF16), conv_w[0], conv_b, w_ffn_down[0].astype(BF16),
               final_norm_g.reshape(1, D_MODEL), tm=tm, fc=256, seq=seq)
    return out.reshape(batch, seq, D_MODEL)
```

```python
import functools

import jax
import jax.numpy as jnp
from jax import lax
from jax.experimental import pallas as pl
from jax.experimental.pallas import tpu as pltpu

F32 = jnp.float32
BF16 = jnp.bfloat16

D_MODEL = 1024
SB_HEAD_DIM = 64
SB_WIDTH = 512
SB_HEADS = SB_WIDTH // SB_HEAD_DIM
GLA_HEADS = 4
GLA_WIDTH = 512
GLA_DV = GLA_WIDTH // GLA_HEADS
GLA_DK = GLA_DV // 2
GLA_KEY_WIDTH = GLA_HEADS * GLA_DK
GLA_GATE_RANK = 16
GLA_GATE_NORMALIZER = 16.0
GLA_CHUNK = 64
D_FF = 2816
CONV_WIDTH = 3
EPS = 1e-6

LANES = 128
SUBLANES = 8
VMEM_LIMIT_BYTES = 56 * 1024 * 1024

P_SB_Q, P_SB_K, P_SB_V = 0, 512, 1024
P_G_Q, P_G_K, P_G_V, P_G_OG = 1536, 1792, 2048, 2560
P_COLS = 3072

LOG2_E = 1.4426950408889634
SB_DEAD_LOG2_WEIGHT = -127.0
SB_MASKED_SCORE = 1.0e4


def _dot(a, b):
    return jnp.dot(a, b, preferred_element_type=F32)


def _dot_nt(a, b):
    return lax.dot_general(a, b, (((1,), (1,)), ((), ())), preferred_element_type=F32)


def _dot_tn(a, b):
    return lax.dot_general(a, b, (((0,), (0,)), ((), ())), preferred_element_type=F32)


def _split_bf16(x):
    hi = x.astype(BF16)
    lo = (x - hi.astype(F32)).astype(BF16)
    return hi, lo


def _log_sigmoid(x):
    return jnp.minimum(x, 0.0) - jnp.log(1.0 + jnp.exp(-jnp.abs(x)))


def _inproj_kernel(x_ref, g_ref, w_ref, wlr_ref, wg_ref, bg_ref, p_ref, la_ref, *, col_chunk, row_parts):
    part = x_ref.shape[0] // row_parts
    rows = [slice(r * part, (r + 1) * part) for r in range(row_parts)]
    y = {}
    for c in range(0, P_COLS, col_chunk):
        for r in range(row_parts):
            if r not in y:
                x = x_ref[rows[r], :]
                ms = jnp.mean(x * x, axis=-1, keepdims=True)
                y[r] = (x * lax.rsqrt(ms + EPS) * g_ref[...]).astype(BF16)
            p_ref[rows[r], c:c + col_chunk] = _dot(y[r], w_ref[:, c:c + col_chunk]).astype(BF16)
            if c == col_chunk:
                g_lr = _dot(y[r], wlr_ref[...])
                logit = _dot(g_lr.astype(BF16), wg_ref[...]) + bg_ref[...]
                la_ref[rows[r], :] = _log_sigmoid(logit) * (1.0 / GLA_GATE_NORMALIZER)


def _inproj(x2d, g, w_main, w_lr, w_gate, b_gate, *, tm):
    T = x2d.shape[0]
    const = lambda i: (0, 0)
    return pl.pallas_call(
        functools.partial(_inproj_kernel, col_chunk=512, row_parts=1),
        grid=(T // tm,),
        in_specs=[
            pl.BlockSpec((tm, D_MODEL), lambda i: (i, 0)),
            pl.BlockSpec((1, D_MODEL), const),
            pl.BlockSpec((D_MODEL, P_COLS), const),
            pl.BlockSpec((D_MODEL, LANES), const),
            pl.BlockSpec((LANES, GLA_KEY_WIDTH), const),
            pl.BlockSpec((1, GLA_KEY_WIDTH), const),
        ],
        out_specs=[
            pl.BlockSpec((tm, P_COLS), lambda i: (i, 0)),
            pl.BlockSpec((tm, GLA_KEY_WIDTH), lambda i: (i, 0)),
        ],
        out_shape=[
            jax.ShapeDtypeStruct((T, P_COLS), BF16),
            jax.ShapeDtypeStruct((T, GLA_KEY_WIDTH), F32),
        ],
        compiler_params=pltpu.CompilerParams(
            dimension_semantics=("arbitrary",), vmem_limit_bytes=VMEM_LIMIT_BYTES),
        name="inproj",
    )(x2d, g, w_main, w_lr, w_gate, b_gate)


def _sb_constants(tile):
    row = jnp.arange(4 * tile)[:, None] % (2 * tile)
    col = jnp.arange(2 * tile)[None, :]
    later2 = ((row > col) & ((row >= tile) == (col >= tile))).astype(BF16)
    t_idx = jnp.arange(tile)[:, None]
    s_idx = jnp.arange(2 * tile)[None, :] % tile
    not_causal = jnp.where(s_idx < t_idx, -SB_MASKED_SCORE * 1e30, SB_MASKED_SCORE).astype(F32)
    r = jnp.arange(2 * LANES)[:, None] % LANES
    c = jnp.arange(LANES)[None, :]
    head_ones = ((r // SB_HEAD_DIM) == (c // SB_HEAD_DIM)).astype(BF16)
    return later2, not_causal, head_ones


def _sb_kernel(q_ref, k_ref, v_ref, g_ref, later2_ref, not_causal_ref, head_ones_ref, o_ref,
               kb_ref, vb_ref, acc_ref, carry_ref, live_ref, *, tile, n_sub):
    assert n_sub >= 2
    n_pairs = SB_HEADS // 2
    pairs = range(n_pairs)
    i = pl.program_id(1)

    @pl.when(i == 0)
    def _():
        low = lax.broadcasted_iota(jnp.int32, (1, SB_WIDTH), 1) % LANES < SB_HEAD_DIM
        for src, dst in ((k_ref, kb_ref), (v_ref, vb_ref)):
            dst[0] = jnp.where(low, src[...], 0)
            dst[1] = jnp.where(low, 0, src[...])

    def both_heads(ref, keys, p):
        cols = slice(p * LANES, (p + 1) * LANES)
        return jnp.concatenate([ref[0, keys, cols], ref[1, keys, cols]], axis=0)

    def window(jobs, diagonal_first):
        units = [(r, p, b) for r, key_starts in jobs for p in pairs for b in range(len(key_starts))]
        keys = {r: [pl.ds(ks, tile) for ks in key_starts] for r, key_starts in jobs}
        zn, logsig, totals, after = {}, {}, {}, {}
        state = {}

        def scores(u):
            r, p, b = u
            zn[u] = _dot_nt(q_ref[r * tile:(r + 1) * tile, p * LANES:(p + 1) * LANES],
                            both_heads(kb_ref, keys[r][b], p))

        def log_terms(u):
            r, p, b = u
            z = zn.pop(u)
            if diagonal_first and b == 0:
                z = jnp.maximum(z, not_causal_ref[...])
            log1m = jnp.minimum(z, 0.0) - jnp.log2(1.0 + jnp.exp2(-jnp.abs(z)))
            logsig[u] = log1m - z
            totals[u] = (log1m[:, 0:1], log1m[:, tile:tile + 1])
            after[u] = _dot(log1m.astype(BF16), later2_ref[0:2 * tile, :])

        def weights_and_values(u):
            r, p, b = u
            if b == 0:
                state["carry"] = None if diagonal_first else carry_ref[r, p]
                state["pv"] = None
                if p == 0:
                    state["live"] = None
            carry = state["carry"]
            aft = after.pop(u)
            first = totals.pop(u)
            total = jnp.concatenate(
                [jnp.broadcast_to(aft[:, 0:1] + first[0], (tile, tile)),
                 jnp.broadcast_to(aft[:, tile:tile + 1] + first[1], (tile, tile))], axis=1)
            if carry is not None:
                aft = aft + carry
                total = total + carry
            state["carry"] = total
            w = jnp.exp2(logsig.pop(u) + aft)
            pv = _dot(w.astype(BF16), both_heads(vb_ref, keys[r][b], p))
            state["pv"] = pv if state["pv"] is None else state["pv"] + pv
            if b == len(keys[r]) - 1:
                carry_ref[r, p] = total
                state["live"] = total if state["live"] is None else jnp.maximum(state["live"], total)
                if diagonal_first:
                    acc_ref[r, p] = state["pv"]
                else:
                    acc_ref[r, p] += state["pv"]
                if p == n_pairs - 1:
                    live_ref[r] = jnp.max(state["live"])

        lag = 2
        n = len(units)
        for t in range(n + 2 * lag):
            if t < n:
                scores(units[t])
            if 0 <= t - lag < n:
                log_terms(units[t - lag])
            if 0 <= t - 2 * lag < n:
                weights_and_values(units[t - 2 * lag])

    def first_blocks(r, n_blocks):
        g0 = (i * n_sub + r) * tile
        return (r, [g0 - b * tile for b in range(n_blocks)])

    @pl.when(i == 0)
    def _():
        window([first_blocks(r, min(r + 1, 3)) for r in range(n_sub)], True)

    @pl.when(i > 0)
    def _():
        window([first_blocks(r, 3) for r in range(n_sub)], True)

    for r in range(n_sub):
        def cond(j, r=r):
            return jnp.logical_and(j >= 0, live_ref[r] > SB_DEAD_LOG2_WEIGHT)

        def body(j, r=r):
            window([(r, [j * tile])], False)
            return j - 1

        lax.while_loop(cond, body, i * n_sub + r - 3)

    outs = [[acc_ref[r, p] for p in pairs] for r in range(n_sub)]
    sumsq = [[_dot(jnp.concatenate(_split_bf16(o * o), axis=1), head_ones_ref[...]) for o in row] for row in outs]
    for r in range(n_sub):
        for p in pairs:
            inv = lax.rsqrt(sumsq[r][p] * (1.0 / SB_HEAD_DIM) + EPS)
            o_ref[r * tile:(r + 1) * tile, p * LANES:(p + 1) * LANES] = (
                outs[r][p] * inv * g_ref[:, p * LANES:(p + 1) * LANES]).astype(BF16)


def _sb_attention(p_act, g, *, batch, seq, tile, n_sub):
    T = batch * seq
    tq = n_sub * tile
    nq = seq // tq
    n_pairs = SB_HEADS // 2
    later2, not_causal, head_ones = _sb_constants(tile)
    return pl.pallas_call(
        functools.partial(_sb_kernel, tile=tile, n_sub=n_sub),
        grid=(batch, nq),
        in_specs=[
            pl.BlockSpec((tq, SB_WIDTH), lambda b, i: (b * nq + i, P_SB_Q // SB_WIDTH)),
            pl.BlockSpec((seq, SB_WIDTH), lambda b, i: (b, P_SB_K // SB_WIDTH)),
            pl.BlockSpec((seq, SB_WIDTH), lambda b, i: (b, P_SB_V // SB_WIDTH)),
            pl.BlockSpec((1, SB_WIDTH), lambda b, i: (0, 0)),
            pl.BlockSpec(later2.shape, lambda b, i: (0, 0)),
            pl.BlockSpec(not_causal.shape, lambda b, i: (0, 0)),
            pl.BlockSpec(head_ones.shape, lambda b, i: (0, 0)),
        ],
        out_specs=pl.BlockSpec((tq, SB_WIDTH), lambda b, i: (b * nq + i, 0)),
        out_shape=jax.ShapeDtypeStruct((T, SB_WIDTH), BF16),
        scratch_shapes=[
            pltpu.VMEM((2, seq, SB_WIDTH), BF16),
            pltpu.VMEM((2, seq, SB_WIDTH), BF16),
            pltpu.VMEM((n_sub, n_pairs, tile, LANES), F32),
            pltpu.VMEM((n_sub, n_pairs, tile, 2 * tile), F32),
            pltpu.SMEM((n_sub,), F32),
        ],
        compiler_params=pltpu.CompilerParams(
            dimension_semantics=("arbitrary", "arbitrary"), vmem_limit_bytes=VMEM_LIMIT_BYTES),
        name="sb",
    )(p_act, p_act, p_act, g, later2, not_causal, head_ones)


def _gla_kernel(q_ref, k_ref, v_ref, og_ref, la_ref, g_ref, o_ref, state_ref, *, seq, step_rows):
    C = GLA_CHUNK
    R = step_rows
    n_chunks = R // C
    pairs = range(GLA_HEADS // 2)
    low_half = lax.broadcasted_iota(jnp.int32, (1, LANES), 1) < GLA_DK
    ri = lax.broadcasted_iota(jnp.int32, (R, R), 0)
    ci = lax.broadcasted_iota(jnp.int32, (R, R), 1)
    same_chunk = (ri // C) == (ci // C)
    causal = same_chunk & (ci <= ri)
    prefix = jnp.where(causal, 1.0, 0.0).astype(BF16)
    whole = jnp.where(same_chunk, 1.0, 0.0).astype(BF16)
    sums_lhs = jnp.concatenate([jnp.concatenate([prefix, prefix], axis=1),
                                jnp.concatenate([whole, whole], axis=1)], axis=0)
    state_ref[...] = jnp.zeros_like(state_ref)

    def front(s):
        rows = pl.ds(pl.multiple_of(s * R, R), R)
        la_hi, la_lo = _split_bf16(la_ref[rows, :])
        sums = _dot(sums_lhs, jnp.concatenate([la_hi, la_lo], axis=0))
        b, b_end = sums[:R], sums[R:]
        q = q_ref[rows, :].astype(F32) * (GLA_DK ** -0.5)
        k = k_ref[rows, :].astype(F32)
        q_dec = q * jnp.exp(b)
        k_inv = (k * jnp.exp(-b)).astype(BF16)
        k_end = k * jnp.exp(b_end - b)
        decay = jnp.exp(b_end)
        q_half, k_end_half = [], []
        for p in pairs:
            ks = slice(p * LANES, (p + 1) * LANES)
            q_half.append((jnp.where(low_half, q_dec[:, ks], 0.0).astype(BF16),
                           jnp.where(low_half, 0.0, q_dec[:, ks]).astype(BF16)))
            k_end_half.append((jnp.where(low_half, k_end[:, ks], 0.0).astype(BF16),
                               jnp.where(low_half, 0.0, k_end[:, ks]).astype(BF16)))
        v = [v_ref[rows, h * GLA_DV:(h + 1) * GLA_DV] for h in range(GLA_HEADS)]
        attn = [jnp.where(causal, _dot_nt(q_half[h // 2][h % 2], k_inv[:, (h // 2) * LANES:(h // 2 + 1) * LANES]),
                          0.0).astype(BF16) for h in range(GLA_HEADS)]
        o_intra = [_dot(attn[h], v[h]) for h in range(GLA_HEADS)]
        kv = [[_dot_tn(jnp.concatenate([v[2 * p][c * C:(c + 1) * C], v[2 * p + 1][c * C:(c + 1) * C]], axis=0),
                       jnp.concatenate([k_end_half[p][0][c * C:(c + 1) * C],
                                        k_end_half[p][1][c * C:(c + 1) * C]], axis=0))
               for c in range(n_chunks)] for p in pairs]
        return rows, q_half, decay, o_intra, kv

    def back(rows, q_half, decay, o_intra, kv):
        o_inter = [[] for _ in range(GLA_HEADS)]
        for p in pairs:
            state_t = state_ref[p]
            for c in range(n_chunks):
                state_b = state_t.astype(BF16)
                for hh in range(2):
                    o_inter[2 * p + hh].append(_dot_nt(q_half[p][hh][c * C:(c + 1) * C], state_b))
                state_t = state_t * decay[c * C:c * C + 1, p * LANES:(p + 1) * LANES] + kv[p][c]
            state_ref[p] = state_t
        for h in range(GLA_HEADS):
            vs = slice(h * GLA_DV, (h + 1) * GLA_DV)
            o = o_intra[h] + jnp.concatenate(o_inter[h], axis=0)
            inv = lax.rsqrt(jnp.mean(o * o, axis=-1, keepdims=True) + EPS)
            gate = og_ref[rows, vs].astype(F32)
            gate = gate / (1.0 + jnp.exp(-gate))
            o_ref[rows, vs] = (o * inv * g_ref[:, vs] * gate).astype(BF16)

    def group(gi, carry):
        s0 = gi * steps_per_group
        pending = front(s0)
        for d in range(1, steps_per_group):
            ahead = front(s0 + d)
            back(*pending)
            pending = ahead
        back(*pending)
        return carry

    n_steps = seq // R
    steps_per_group = 8 if n_steps % 8 == 0 else 1
    lax.fori_loop(0, n_steps // steps_per_group, group, 0)


def _gla(p_act, log_a, g, *, batch, seq):
    T = batch * seq
    n_pairs = GLA_HEADS // 2
    return pl.pallas_call(
        functools.partial(_gla_kernel, seq=seq, step_rows=min(256, seq)),
        grid=(batch,),
        in_specs=[
            pl.BlockSpec((seq, GLA_KEY_WIDTH), lambda b: (b, P_G_Q // GLA_KEY_WIDTH)),
            pl.BlockSpec((seq, GLA_KEY_WIDTH), lambda b: (b, P_G_K // GLA_KEY_WIDTH)),
            pl.BlockSpec((seq, GLA_WIDTH), lambda b: (b, P_G_V // GLA_WIDTH)),
            pl.BlockSpec((seq, GLA_WIDTH), lambda b: (b, P_G_OG // GLA_WIDTH)),
            pl.BlockSpec((seq, GLA_KEY_WIDTH), lambda b: (b, 0)),
            pl.BlockSpec((1, GLA_WIDTH), lambda b: (0, 0)),
        ],
        out_specs=pl.BlockSpec((seq, GLA_WIDTH), lambda b: (b, 0)),
        out_shape=jax.ShapeDtypeStruct((T, GLA_WIDTH), BF16),
        scratch_shapes=[pltpu.VMEM((n_pairs, GLA_DV, LANES), F32)],
        compiler_params=pltpu.CompilerParams(
            dimension_semantics=("arbitrary",), vmem_limit_bytes=VMEM_LIMIT_BYTES),
        name="gla",
    )(p_act, p_act, p_act, p_act, log_a, g)


def _rms(x, g):
    ms = jnp.mean(x * x, axis=-1, keepdims=True)
    return x * lax.rsqrt(ms + EPS) * g


def _ffn_kernel(x_ref, osb_ref, ogla_ref, wo_sb_ref, wo_gla_ref, g2_ref, wup_ref, cw_ref, cb_ref,
                wdn_ref, gf_ref, out_ref, x1_ref, h2_ref, acc_ref, u00_ref, u01_ref, u10_ref, u11_ref,
                halo_ref,
                *, tm, fc, tiles_per_seq):
    t = pl.program_id(0)
    n_chunks = D_FF // fc
    H = SUBLANES

    @pl.when(t % tiles_per_seq == 0)
    def _():
        halo_ref[...] = jnp.zeros_like(halo_ref)

    x1 = x_ref[...] + _dot(osb_ref[...], wo_sb_ref[...]) + _dot(ogla_ref[...], wo_gla_ref[...])
    x1_ref[...] = x1
    h2_ref[...] = _rms(x1, g2_ref[...]).astype(BF16)

    u_bufs = ((u00_ref, u01_ref), (u10_ref, u11_ref))

    def cols_of(part, c):
        return slice(part * D_FF + c * fc, part * D_FF + (c + 1) * fc)

    def up_project(c, part):
        buf = u_bufs[c % 2][part]
        u = _dot(h2_ref[...], wup_ref[:, cols_of(part, c)])
        buf[0:H, :] = halo_ref[part, c]
        buf[H:H + tm, :] = u
        halo_ref[part, c] = u[tm - H:tm, :]

    def conv_gate_down(c, r0, nrows):
        conv = []
        for part in range(2):
            buf = u_bufs[c % 2][part]
            cw = cw_ref[:, cols_of(part, c)]
            win = buf[r0:H + r0 + nrows, :]
            conv.append(cb_ref[:, cols_of(part, c)] + cw[2:3, :] * win[H:, :]
                        + cw[1:2, :] * pltpu.roll(win, 1, axis=0)[H:, :]
                        + cw[0:1, :] * pltpu.roll(win, 2, axis=0)[H:, :])
        a, val = conv
        half_a = 0.5 * a
        gated = ((half_a + half_a * jnp.tanh(half_a)) * val).astype(BF16)
        down = _dot(gated, wdn_ref[c * fc:(c + 1) * fc, :])
        if c == 0:
            acc_ref[r0:r0 + nrows, :] = down
        else:
            acc_ref[r0:r0 + nrows, :] += down

    half = tm // 2
    up_project(0, 0)
    up_project(0, 1)
    for c in range(n_chunks):
        for part in range(2):
            if c + 1 < n_chunks:
                up_project(c + 1, part)
            conv_gate_down(c, part * half, half)

    out_ref[...] = _rms(x1_ref[...] + acc_ref[...], gf_ref[...])


def _ffn(x2d, o_sb, o_gla, wo_sb, wo_gla, g2, w_up, conv_w, conv_b, w_down, gf, *, tm, fc, seq):
    T = x2d.shape[0]
    n_chunks = D_FF // fc
    const = lambda i: (0, 0)
    resident = functools.partial(pl.BlockSpec, index_map=const, pipeline_mode=pl.Buffered(1))
    return pl.pallas_call(
        functools.partial(_ffn_kernel, tm=tm, fc=fc, tiles_per_seq=seq // tm),
        grid=(T // tm,),
        in_specs=[
            pl.BlockSpec((tm, D_MODEL), lambda i: (i, 0)),
            pl.BlockSpec((tm, SB_WIDTH), lambda i: (i, 0)),
            pl.BlockSpec((tm, GLA_WIDTH), lambda i: (i, 0)),
            resident((SB_WIDTH, D_MODEL)),
            resident((GLA_WIDTH, D_MODEL)),
            resident((1, D_MODEL)),
            resident((D_MODEL, 2 * D_FF)),
            resident((CONV_WIDTH, 2 * D_FF)),
            resident((1, 2 * D_FF)),
            resident((D_FF, D_MODEL)),
            resident((1, D_MODEL)),
        ],
        out_specs=pl.BlockSpec((tm, D_MODEL), lambda i: (i, 0)),
        out_shape=jax.ShapeDtypeStruct((T, D_MODEL), F32),
        scratch_shapes=[
            pltpu.VMEM((tm, D_MODEL), F32),
            pltpu.VMEM((tm, D_MODEL), BF16),
            pltpu.VMEM((tm, D_MODEL), F32),
            pltpu.VMEM((SUBLANES + tm, fc), F32),
            pltpu.VMEM((SUBLANES + tm, fc), F32),
            pltpu.VMEM((SUBLANES + tm, fc), F32),
            pltpu.VMEM((SUBLANES + tm, fc), F32),
            pltpu.VMEM((2, n_chunks, SUBLANES, fc), F32),
        ],
        compiler_params=pltpu.CompilerParams(
            dimension_semantics=("arbitrary",), vmem_limit_bytes=VMEM_LIMIT_BYTES),
        name="ffn",
    )(x2d, o_sb, o_gla, wo_sb, wo_gla, g2, w_up, conv_w, conv_b, w_down, gf)


def kernel(x, attn_norm_g, w_in, w_gate_up, b_gate_up, sb_out_g, gla_out_g, w_out, ffn_norm_g,
           w_ffn_up, conv_w, conv_b, w_ffn_down, final_norm_g):
    batch, seq, d_model = x.shape
    assert d_model == D_MODEL and attn_norm_g.shape[0] == 1
    T = batch * seq
    x2d = x.reshape(T, D_MODEL)

    lr0 = P_G_OG
    w_in0 = w_in[0]
    sb_q_scale = -(SB_HEAD_DIM ** -0.5) * LOG2_E
    w_main = jnp.concatenate([w_in0[:, :SB_WIDTH] * sb_q_scale, w_in0[:, SB_WIDTH:lr0],
                              w_in0[:, lr0 + GLA_GATE_RANK:]], axis=1).astype(BF16)
    w_lr = jnp.pad(w_in0[:, lr0:lr0 + GLA_GATE_RANK], ((0, 0), (0, LANES - GLA_GATE_RANK))).astype(BF16)
    w_gate = jnp.pad(w_gate_up[0], ((0, LANES - GLA_GATE_RANK), (0, 0))).astype(BF16)

    tm = min(512, seq)
    p_act, log_a = _inproj(x2d, attn_norm_g, w_main, w_lr, w_gate, b_gate_up, tm=tm)
    o_sb = _sb_attention(p_act, sb_out_g, batch=batch, seq=seq, tile=LANES, n_sub=4)
    o_gla = _gla(p_act, log_a, gla_out_g, batch=batch, seq=seq)
    out = _ffn(x2d, o_sb, o_gla,
               w_out[0, :SB_WIDTH].astype(BF16), w_out[0, SB_WIDTH:].astype(BF16),
               ffn_norm_g, w_ffn_up[0].astype(BF16), conv_w[0], conv_b, w_ffn_down[0].astype(BF16),
               final_norm_g.reshape(1, D_MODEL), tm=tm, fc=256, seq=seq)
    return out.reshape(batch, seq, D_MODEL)
```

```python
import functools

import jax
import jax.numpy as jnp
from jax import lax
from jax.experimental import pallas as pl
from jax.experimental.pallas import tpu as pltpu

F32 = jnp.float32
BF16 = jnp.bfloat16

D_MODEL = 1024
SB_HEAD_DIM = 64
SB_WIDTH = 512
SB_HEADS = SB_WIDTH // SB_HEAD_DIM
GLA_HEADS = 4
GLA_WIDTH = 512
GLA_DV = GLA_WIDTH // GLA_HEADS
GLA_DK = GLA_DV // 2
GLA_KEY_WIDTH = GLA_HEADS * GLA_DK
GLA_GATE_RANK = 16
GLA_GATE_NORMALIZER = 16.0
GLA_CHUNK = 64
D_FF = 2816
CONV_WIDTH = 3
EPS = 1e-6

LANES = 128
SUBLANES = 8
VMEM_LIMIT_BYTES = 56 * 1024 * 1024

P_SB_Q, P_SB_K, P_SB_V = 0, 512, 1024
P_G_Q, P_G_K, P_G_V, P_G_OG = 1536, 1792, 2048, 2560
P_COLS = 3072

LOG2_E = 1.4426950408889634
SB_DEAD_LOG2_WEIGHT = -127.0
SB_MASKED_SCORE = 1.0e4


def _dot(a, b):
    return jnp.dot(a, b, preferred_element_type=F32)


def _dot_nt(a, b):
    return lax.dot_general(a, b, (((1,), (1,)), ((), ())), preferred_element_type=F32)


def _dot_tn(a, b):
    return lax.dot_general(a, b, (((0,), (0,)), ((), ())), preferred_element_type=F32)


def _split_bf16(x):
    hi = x.astype(BF16)
    lo = (x - hi.astype(F32)).astype(BF16)
    return hi, lo


def _log_sigmoid(x):
    return jnp.minimum(x, 0.0) - jnp.log(1.0 + jnp.exp(-jnp.abs(x)))


def _inproj_kernel(x_ref, g_ref, w_ref, wlr_ref, wg_ref, bg_ref, p_ref, la_ref, *, col_chunk, row_parts):
    part = x_ref.shape[0] // row_parts
    rows = [slice(r * part, (r + 1) * part) for r in range(row_parts)]
    y = {}
    for c in range(0, P_COLS, col_chunk):
        for r in range(row_parts):
            if r not in y:
                x = x_ref[rows[r], :]
                ms = jnp.mean(x * x, axis=-1, keepdims=True)
                y[r] = (x * lax.rsqrt(ms + EPS) * g_ref[...]).astype(BF16)
            p_ref[rows[r], c:c + col_chunk] = _dot(y[r], w_ref[:, c:c + col_chunk]).astype(BF16)
            if c == col_chunk:
                g_lr = _dot(y[r], wlr_ref[...])
                logit = _dot(g_lr.astype(BF16), wg_ref[...]) + bg_ref[...]
                la_ref[rows[r], :] = _log_sigmoid(logit) * (1.0 / GLA_GATE_NORMALIZER)


def _inproj(x2d, g, w_main, w_lr, w_gate, b_gate, *, tm):
    T = x2d.shape[0]
    const = lambda i: (0, 0)
    return pl.pallas_call(
        functools.partial(_inproj_kernel, col_chunk=512, row_parts=1),
        grid=(T // tm,),
        in_specs=[
            pl.BlockSpec((tm, D_MODEL), lambda i: (i, 0)),
            pl.BlockSpec((1, D_MODEL), const),
            pl.BlockSpec((D_MODEL, P_COLS), const),
            pl.BlockSpec((D_MODEL, LANES), const),
            pl.BlockSpec((LANES, GLA_KEY_WIDTH), const),
            pl.BlockSpec((1, GLA_KEY_WIDTH), const),
        ],
        out_specs=[
            pl.BlockSpec((tm, P_COLS), lambda i: (i, 0)),
            pl.BlockSpec((tm, GLA_KEY_WIDTH), lambda i: (i, 0)),
        ],
        out_shape=[
            jax.ShapeDtypeStruct((T, P_COLS), BF16),
            jax.ShapeDtypeStruct((T, GLA_KEY_WIDTH), F32),
        ],
        compiler_params=pltpu.CompilerParams(
            dimension_semantics=("arbitrary",), vmem_limit_bytes=VMEM_LIMIT_BYTES),
        name="inproj",
    )(x2d, g, w_main, w_lr, w_gate, b_gate)


def _sb_constants(tile):
    row = jnp.arange(4 * tile)[:, None] % (2 * tile)
    col = jnp.arange(2 * tile)[None, :]
    later2 = ((row > col) & ((row >= tile) == (col >= tile))).astype(BF16)
    t_idx = jnp.arange(tile)[:, None]
    s_idx = jnp.arange(2 * tile)[None, :] % tile
    not_causal = jnp.where(s_idx < t_idx, -SB_MASKED_SCORE * 1e30, SB_MASKED_SCORE).astype(F32)
    r = jnp.arange(2 * LANES)[:, None] % LANES
    c = jnp.arange(LANES)[None, :]
    head_ones = ((r // SB_HEAD_DIM) == (c // SB_HEAD_DIM)).astype(BF16)
    return later2, not_causal, head_ones


def _sb_kernel(q_ref, k_ref, v_ref, g_ref, later2_ref, not_causal_ref, head_ones_ref, o_ref,
               kb_ref, vb_ref, acc_ref, carry_ref, live_ref, *, tile, n_sub):
    assert n_sub >= 2
    n_pairs = SB_HEADS // 2
    pairs = range(n_pairs)
    i = pl.program_id(1)

    @pl.when(i == 0)
    def _():
        low = lax.broadcasted_iota(jnp.int32, (1, SB_WIDTH), 1) % LANES < SB_HEAD_DIM
        for src, dst in ((k_ref, kb_ref), (v_ref, vb_ref)):
            dst[0] = jnp.where(low, src[...], 0)
            dst[1] = jnp.where(low, 0, src[...])

    def both_heads(ref, keys, p):
        cols = slice(p * LANES, (p + 1) * LANES)
        return jnp.concatenate([ref[0, keys, cols], ref[1, keys, cols]], axis=0)

    def window(jobs, diagonal_first):
        units = [(r, p, b) for r, key_starts in jobs for p in pairs for b in range(len(key_starts))]
        keys = {r: [pl.ds(ks, tile) for ks in key_starts] for r, key_starts in jobs}
        zn, logsig, totals, after = {}, {}, {}, {}
        state = {}

        def scores(u):
            r, p, b = u
            zn[u] = _dot_nt(q_ref[r * tile:(r + 1) * tile, p * LANES:(p + 1) * LANES],
                            both_heads(kb_ref, keys[r][b], p))

        def log_terms(u):
            r, p, b = u
            z = zn.pop(u)
            if diagonal_first and b == 0:
                z = jnp.maximum(z, not_causal_ref[...])
            log1m = jnp.minimum(z, 0.0) - jnp.log2(1.0 + jnp.exp2(-jnp.abs(z)))
            logsig[u] = log1m - z
            totals[u] = (log1m[:, 0:1], log1m[:, tile:tile + 1])
            after[u] = _dot(log1m.astype(BF16), later2_ref[0:2 * tile, :])

        def weights_and_values(u):
            r, p, b = u
            if b == 0:
                state["carry"] = None if diagonal_first else carry_ref[r, p]
                state["pv"] = None
                if p == 0:
                    state["live"] = None
            carry = state["carry"]
            aft = after.pop(u)
            first = totals.pop(u)
            total = jnp.concatenate(
                [jnp.broadcast_to(aft[:, 0:1] + first[0], (tile, tile)),
                 jnp.broadcast_to(aft[:, tile:tile + 1] + first[1], (tile, tile))], axis=1)
            if carry is not None:
                aft = aft + carry
                total = total + carry
            state["carry"] = total
            w = jnp.exp2(logsig.pop(u) + aft)
            pv = _dot(w.astype(BF16), both_heads(vb_ref, keys[r][b], p))
            state["pv"] = pv if state["pv"] is None else state["pv"] + pv
            if b == len(keys[r]) - 1:
                carry_ref[r, p] = total
                state["live"] = total if state["live"] is None else jnp.maximum(state["live"], total)
                if diagonal_first:
                    acc_ref[r, p] = state["pv"]
                else:
                    acc_ref[r, p] += state["pv"]
                if p == n_pairs - 1:
                    live_ref[r] = jnp.max(state["live"])

        lag = 2
        n = len(units)
        for t in range(n + 2 * lag):
            if t < n:
                scores(units[t])
            if 0 <= t - lag < n:
                log_terms(units[t - lag])
            if 0 <= t - 2 * lag < n:
                weights_and_values(units[t - 2 * lag])

    def first_blocks(r, n_blocks):
        g0 = (i * n_sub + r) * tile
        return (r, [g0 - b * tile for b in range(n_blocks)])

    @pl.when(i == 0)
    def _():
        window([first_blocks(r, min(r + 1, 3)) for r in range(n_sub)], True)

    @pl.when(i > 0)
    def _():
        window([first_blocks(r, 3) for r in range(n_sub)], True)

    for r in range(n_sub):
        def cond(j, r=r):
            return jnp.logical_and(j >= 0, live_ref[r] > SB_DEAD_LOG2_WEIGHT)

        def body(j, r=r):
            window([(r, [j * tile])], False)
            return j - 1

        lax.while_loop(cond, body, i * n_sub + r - 3)

    outs = [[acc_ref[r, p] for p in pairs] for r in range(n_sub)]
    sumsq = [[_dot(jnp.concatenate(_split_bf16(o * o), axis=1), head_ones_ref[...]) for o in row] for row in outs]
    for r in range(n_sub):
        for p in pairs:
            inv = lax.rsqrt(sumsq[r][p] * (1.0 / SB_HEAD_DIM) + EPS)
            o_ref[r * tile:(r + 1) * tile, p * LANES:(p + 1) * LANES] = (
                outs[r][p] * inv * g_ref[:, p * LANES:(p + 1) * LANES]).astype(BF16)


def _sb_attention(p_act, g, *, batch, seq, tile, n_sub):
    T = batch * seq
    tq = n_sub * tile
    nq = seq // tq
    n_pairs = SB_HEADS // 2
    later2, not_causal, head_ones = _sb_constants(tile)
    return pl.pallas_call(
        functools.partial(_sb_kernel, tile=tile, n_sub=n_sub),
        grid=(batch, nq),
        in_specs=[
            pl.BlockSpec((tq, SB_WIDTH), lambda b, i: (b * nq + i, P_SB_Q // SB_WIDTH)),
            pl.BlockSpec((seq, SB_WIDTH), lambda b, i: (b, P_SB_K // SB_WIDTH)),
            pl.BlockSpec((seq, SB_WIDTH), lambda b, i: (b, P_SB_V // SB_WIDTH)),
            pl.BlockSpec((1, SB_WIDTH), lambda b, i: (0, 0)),
            pl.BlockSpec(later2.shape, lambda b, i: (0, 0)),
            pl.BlockSpec(not_causal.shape, lambda b, i: (0, 0)),
            pl.BlockSpec(head_ones.shape, lambda b, i: (0, 0)),
        ],
        out_specs=pl.BlockSpec((tq, SB_WIDTH), lambda b, i: (b * nq + i, 0)),
        out_shape=jax.ShapeDtypeStruct((T, SB_WIDTH), BF16),
        scratch_shapes=[
            pltpu.VMEM((2, seq, SB_WIDTH), BF16),
            pltpu.VMEM((2, seq, SB_WIDTH), BF16),
            pltpu.VMEM((n_sub, n_pairs, tile, LANES), F32),
            pltpu.VMEM((n_sub, n_pairs, tile, 2 * tile), F32),
            pltpu.SMEM((n_sub,), F32),
        ],
        compiler_params=pltpu.CompilerParams(
            dimension_semantics=("arbitrary", "arbitrary"), vmem_limit_bytes=VMEM_LIMIT_BYTES),
        name="sb",
    )(p_act, p_act, p_act, g, later2, not_causal, head_ones)


def _gla_kernel(q_ref, k_ref, v_ref, og_ref, la_ref, g_ref, o_ref, state_ref, *, seq, step_rows):
    C = GLA_CHUNK
    R = step_rows
    n_chunks = R // C
    pairs = range(GLA_HEADS // 2)
    low_half = lax.broadcasted_iota(jnp.int32, (1, LANES), 1) < GLA_DK
    ri = lax.broadcasted_iota(jnp.int32, (R, R), 0)
    ci = lax.broadcasted_iota(jnp.int32, (R, R), 1)
    same_chunk = (ri // C) == (ci // C)
    causal = same_chunk & (ci <= ri)
    prefix = jnp.where(causal, 1.0, 0.0).astype(BF16)
    prefix2 = jnp.concatenate([prefix, prefix], axis=1)
    state_ref[...] = jnp.zeros_like(state_ref)

    def front(s):
        rows = pl.ds(pl.multiple_of(s * R, R), R)
        la_hi, la_lo = _split_bf16(la_ref[rows, :])
        b = _dot(prefix2, jnp.concatenate([la_hi, la_lo], axis=0))
        b_last = [b[c * C + C - 1:(c + 1) * C, :] for c in range(n_chunks)]
        b_end = jnp.concatenate([jnp.broadcast_to(r, (C, r.shape[1])) for r in b_last], axis=0)
        q = q_ref[rows, :].astype(F32) * (GLA_DK ** -0.5)
        k = k_ref[rows, :].astype(F32)
        q_dec = q * jnp.exp(b)
        k_inv = (k * jnp.exp(-b)).astype(BF16)
        k_end = k * jnp.exp(b_end - b)
        decay = [jnp.exp(r) for r in b_last]
        q_half, k_end_half = [], []
        for p in pairs:
            ks = slice(p * LANES, (p + 1) * LANES)
            q_half.append((jnp.where(low_half, q_dec[:, ks], 0.0).astype(BF16),
                           jnp.where(low_half, 0.0, q_dec[:, ks]).astype(BF16)))
            k_end_half.append((jnp.where(low_half, k_end[:, ks], 0.0).astype(BF16),
                               jnp.where(low_half, 0.0, k_end[:, ks]).astype(BF16)))
        v = [v_ref[rows, h * GLA_DV:(h + 1) * GLA_DV] for h in range(GLA_HEADS)]
        attn = [jnp.where(causal, _dot_nt(q_half[h // 2][h % 2], k_inv[:, (h // 2) * LANES:(h // 2 + 1) * LANES]),
                          0.0).astype(BF16) for h in range(GLA_HEADS)]
        o_intra = [_dot(attn[h], v[h]) for h in range(GLA_HEADS)]
        kv = [[_dot_tn(jnp.concatenate([v[2 * p][c * C:(c + 1) * C], v[2 * p + 1][c * C:(c + 1) * C]], axis=0),
                       jnp.concatenate([k_end_half[p][0][c * C:(c + 1) * C],
                                        k_end_half[p][1][c * C:(c + 1) * C]], axis=0))
               for c in range(n_chunks)] for p in pairs]
        return rows, q_half, decay, o_intra, kv

    def back(rows, q_half, decay, o_intra, kv):
        o_inter = [[] for _ in range(GLA_HEADS)]
        for p in pairs:
            state_t = state_ref[p]
            for c in range(n_chunks):
                state_b = state_t.astype(BF16)
                for hh in range(2):
                    o_inter[2 * p + hh].append(_dot_nt(q_half[p][hh][c * C:(c + 1) * C], state_b))
                state_t = state_t * decay[c][:, p * LANES:(p + 1) * LANES] + kv[p][c]
            state_ref[p] = state_t
        for h in range(GLA_HEADS):
            vs = slice(h * GLA_DV, (h + 1) * GLA_DV)
            o = o_intra[h] + jnp.concatenate(o_inter[h], axis=0)
            inv = lax.rsqrt(jnp.mean(o * o, axis=-1, keepdims=True) + EPS)
            gate = og_ref[rows, vs].astype(F32)
            gate = gate / (1.0 + jnp.exp(-gate))
            o_ref[rows, vs] = (o * inv * g_ref[:, vs] * gate).astype(BF16)

    def group(gi, carry):
        s0 = gi * steps_per_group
        pending = front(s0)
        for d in range(1, steps_per_group):
            ahead = front(s0 + d)
            back(*pending)
            pending = ahead
        back(*pending)
        return carry

    n_steps = seq // R
    steps_per_group = 8 if n_steps % 8 == 0 else 1
    lax.fori_loop(0, n_steps // steps_per_group, group, 0)


def _gla(p_act, log_a, g, *, batch, seq):
    T = batch * seq
    n_pairs = GLA_HEADS // 2
    return pl.pallas_call(
        functools.partial(_gla_kernel, seq=seq, step_rows=min(256, seq)),
        grid=(batch,),
        in_specs=[
            pl.BlockSpec((seq, GLA_KEY_WIDTH), lambda b: (b, P_G_Q // GLA_KEY_WIDTH)),
            pl.BlockSpec((seq, GLA_KEY_WIDTH), lambda b: (b, P_G_K // GLA_KEY_WIDTH)),
            pl.BlockSpec((seq, GLA_WIDTH), lambda b: (b, P_G_V // GLA_WIDTH)),
            pl.BlockSpec((seq, GLA_WIDTH), lambda b: (b, P_G_OG // GLA_WIDTH)),
            pl.BlockSpec((seq, GLA_KEY_WIDTH), lambda b: (b, 0)),
            pl.BlockSpec((1, GLA_WIDTH), lambda b: (0, 0)),
        ],
        out_specs=pl.BlockSpec((seq, GLA_WIDTH), lambda b: (b, 0)),
        out_shape=jax.ShapeDtypeStruct((T, GLA_WIDTH), BF16),
        scratch_shapes=[pltpu.VMEM((n_pairs, GLA_DV, LANES), F32)],
        compiler_params=pltpu.CompilerParams(
            dimension_semantics=("arbitrary",), vmem_limit_bytes=VMEM_LIMIT_BYTES),
        name="gla",
    )(p_act, p_act, p_act, p_act, log_a, g)


def _rms(x, g):
    ms = jnp.mean(x * x, axis=-1, keepdims=True)
    return x * lax.rsqrt(ms + EPS) * g


def _ffn_kernel(x_ref, osb_ref, ogla_ref, wo_sb_ref, wo_gla_ref, g2_ref, wup_ref, cw_ref, cb_ref,
                wdn_ref, gf_ref, out_ref, x1_ref, h2_ref, acc_ref, u00_ref, u01_ref, u10_ref, u11_ref,
                halo_ref,
                *, tm, fc, tiles_per_seq):
    t = pl.program_id(0)
    n_chunks = D_FF // fc
    H = SUBLANES

    @pl.when(t % tiles_per_seq == 0)
    def _():
        halo_ref[...] = jnp.zeros_like(halo_ref)

    x1 = x_ref[...] + _dot(osb_ref[...], wo_sb_ref[...]) + _dot(ogla_ref[...], wo_gla_ref[...])
    x1_ref[...] = x1
    h2_ref[...] = _rms(x1, g2_ref[...]).astype(BF16)

    u_bufs = ((u00_ref, u01_ref), (u10_ref, u11_ref))

    def cols_of(part, c):
        return slice(part * D_FF + c * fc, part * D_FF + (c + 1) * fc)

    def up_project(c, part):
        buf = u_bufs[c % 2][part]
        u = _dot(h2_ref[...], wup_ref[:, cols_of(part, c)])
        buf[0:H, :] = halo_ref[part, c]
        buf[H:H + tm, :] = u
        halo_ref[part, c] = u[tm - H:tm, :]

    def conv_gate_down(c, r0, nrows):
        conv = []
        for part in range(2):
            buf = u_bufs[c % 2][part]
            cw = cw_ref[:, cols_of(part, c)]
            win = buf[r0:H + r0 + nrows, :]
            conv.append(cb_ref[:, cols_of(part, c)] + cw[2:3, :] * win[H:, :]
                        + cw[1:2, :] * pltpu.roll(win, 1, axis=0)[H:, :]
                        + cw[0:1, :] * pltpu.roll(win, 2, axis=0)[H:, :])
        a, val = conv
        half_a = 0.5 * a
        gated = ((half_a + half_a * jnp.tanh(half_a)) * val).astype(BF16)
        down = _dot(gated, wdn_ref[c * fc:(c + 1) * fc, :])
        if c == 0:
            acc_ref[r0:r0 + nrows, :] = down
        else:
            acc_ref[r0:r0 + nrows, :] += down

    half = tm // 2
    up_project(0, 0)
    up_project(0, 1)
    for c in range(n_chunks):
        for part in range(2):
            if c + 1 < n_chunks:
                up_project(c + 1, part)
            conv_gate_down(c, part * half, half)

    out_ref[...] = _rms(x1_ref[...] + acc_ref[...], gf_ref[...])


def _ffn(x2d, o_sb, o_gla, wo_sb, wo_gla, g2, w_up, conv_w, conv_b, w_down, gf, *, tm, fc, seq):
    T = x2d.shape[0]
    n_chunks = D_FF // fc
    const = lambda i: (0, 0)
    resident = functools.partial(pl.BlockSpec, index_map=const, pipeline_mode=pl.Buffered(1))
    return pl.pallas_call(
        functools.partial(_ffn_kernel, tm=tm, fc=fc, tiles_per_seq=seq // tm),
        grid=(T // tm,),
        in_specs=[
            pl.BlockSpec((tm, D_MODEL), lambda i: (i, 0)),
            pl.BlockSpec((tm, SB_WIDTH), lambda i: (i, 0)),
            pl.BlockSpec((tm, GLA_WIDTH), lambda i: (i, 0)),
            resident((SB_WIDTH, D_MODEL)),
            resident((GLA_WIDTH, D_MODEL)),
            resident((1, D_MODEL)),
            resident((D_MODEL, 2 * D_FF)),
            resident((CONV_WIDTH, 2 * D_FF)),
            resident((1, 2 * D_FF)),
            resident((D_FF, D_MODEL)),
            resident((1, D_MODEL)),
        ],
        out_specs=pl.BlockSpec((tm, D_MODEL), lambda i: (i, 0)),
        out_shape=jax.ShapeDtypeStruct((T, D_MODEL), F32),
        scratch_shapes=[
            pltpu.VMEM((tm, D_MODEL), F32),
            pltpu.VMEM((tm, D_MODEL), BF16),
            pltpu.VMEM((tm, D_MODEL), F32),
            pltpu.VMEM((SUBLANES + tm, fc), F32),
            pltpu.VMEM((SUBLANES + tm, fc), F32),
            pltpu.VMEM((SUBLANES + tm, fc), F32),
            pltpu.VMEM((SUBLANES + tm, fc), F32),
            pltpu.VMEM((2, n_chunks, SUBLANES, fc), F32),
        ],
        compiler_params=pltpu.CompilerParams(
            dimension_semantics=("arbitrary",), vmem_limit_bytes=VMEM_LIMIT_BYTES),
        name="ffn",
    )(x2d, o_sb, o_gla, wo_sb, wo_gla, g2, w_up, conv_w, conv_b, w_down, gf)


def kernel(x, attn_norm_g, w_in, w_gate_up, b_gate_up, sb_out_g, gla_out_g, w_out, ffn_norm_g,
           w_ffn_up, conv_w, conv_b, w_ffn_down, final_norm_g):
    batch, seq, d_model = x.shape
    assert d_model == D_MODEL and attn_norm_g.shape[0] == 1
    T = batch * seq
    x2d = x.reshape(T, D_MODEL)

    lr0 = P_G_OG
    w_in0 = w_in[0]
    sb_q_scale = -(SB_HEAD_DIM ** -0.5) * LOG2_E
    w_main = jnp.concatenate([w_in0[:, :SB_WIDTH] * sb_q_scale, w_in0[:, SB_WIDTH:lr0],
                              w_in0[:, lr0 + GLA_GATE_RANK:]], axis=1).astype(BF16)
    w_lr = jnp.pad(w_in0[:, lr0:lr0 + GLA_GATE_RANK], ((0, 0), (0, LANES - GLA_GATE_RANK))).astype(BF16)
    w_gate = jnp.pad(w_gate_up[0], ((0, LANES - GLA_GATE_RANK), (0, 0))).astype(BF16)

    tm = min(512, seq)
    p_act, log_a = _inproj(x2d, attn_norm_g, w_main, w_lr, w_gate, b_gate_up, tm=tm)
    o_sb = _sb_attention(p_act, sb_out_g, batch=batch, seq=seq, tile=LANES, n_sub=8)
    o_gla = _gla(p_act, log_a, gla_out_g, batch=batch, seq=seq)
    out = _ffn(x2d, o_sb, o_gla,
               w_out[0, :SB_WIDTH].astype(BF16), w_out[0, SB_WIDTH:].astype(BF16),
               ffn_norm_g, w_ffn_up[0].astype(BF16), conv_w[0], conv_b, w_ffn_down[0].astype(BF16),
               final_norm_g.reshape(1, D_MODEL), tm=tm, fc=256, seq=seq)
    return out.reshape(batch, seq, D_MODEL)
```

```python
import functools

import jax
import jax.numpy as jnp
from jax import lax
from jax.experimental import pallas as pl
from jax.experimental.pallas import tpu as pltpu

F32 = jnp.float32
BF16 = jnp.bfloat16

D_MODEL = 1024
SB_HEAD_DIM = 64
SB_WIDTH = 512
SB_HEADS = SB_WIDTH // SB_HEAD_DIM
GLA_HEADS = 4
GLA_WIDTH = 512
GLA_DV = GLA_WIDTH // GLA_HEADS
GLA_DK = GLA_DV // 2
GLA_KEY_WIDTH = GLA_HEADS * GLA_DK
GLA_GATE_RANK = 16
GLA_GATE_NORMALIZER = 16.0
GLA_CHUNK = 64
D_FF = 2816
CONV_WIDTH = 3
EPS = 1e-6

LANES = 128
SUBLANES = 8
VMEM_LIMIT_BYTES = 56 * 1024 * 1024

P_SB_Q, P_SB_K, P_SB_V = 0, 512, 1024
P_G_Q, P_G_K, P_G_V, P_G_OG = 1536, 1792, 2048, 2560
P_COLS = 3072

LOG2_E = 1.4426950408889634
SB_DEAD_LOG2_WEIGHT = -127.0
SB_MASKED_SCORE = 1.0e4


def _dot(a, b):
    return jnp.dot(a, b, preferred_element_type=F32)


def _dot_nt(a, b):
    return lax.dot_general(a, b, (((1,), (1,)), ((), ())), preferred_element_type=F32)


def _dot_tn(a, b):
    return lax.dot_general(a, b, (((0,), (0,)), ((), ())), preferred_element_type=F32)


def _split_bf16(x):
    hi = x.astype(BF16)
    lo = (x - hi.astype(F32)).astype(BF16)
    return hi, lo


def _log_sigmoid(x):
    return jnp.minimum(x, 0.0) - jnp.log(1.0 + jnp.exp(-jnp.abs(x)))


def _inproj_kernel(x_ref, g_ref, w_ref, wlr_ref, wg_ref, bg_ref, p_ref, la_ref, *, col_chunk):
    x = x_ref[...]
    ms = jnp.mean(x * x, axis=-1, keepdims=True)
    y = (x * lax.rsqrt(ms + EPS) * g_ref[...]).astype(BF16)
    for c in range(0, P_COLS, col_chunk):
        p_ref[:, c:c + col_chunk] = _dot(y, w_ref[:, c:c + col_chunk]).astype(BF16)
    g_lr = _dot(y, wlr_ref[...])
    logit = _dot(g_lr.astype(BF16), wg_ref[...]) + bg_ref[...]
    la_ref[...] = _log_sigmoid(logit) * (1.0 / GLA_GATE_NORMALIZER)


def _inproj(x2d, g, w_main, w_lr, w_gate, b_gate, *, tm):
    T = x2d.shape[0]
    const = lambda i: (0, 0)
    return pl.pallas_call(
        functools.partial(_inproj_kernel, col_chunk=512),
        grid=(T // tm,),
        in_specs=[
            pl.BlockSpec((tm, D_MODEL), lambda i: (i, 0)),
            pl.BlockSpec((1, D_MODEL), const),
            pl.BlockSpec((D_MODEL, P_COLS), const),
            pl.BlockSpec((D_MODEL, LANES), const),
            pl.BlockSpec((LANES, GLA_KEY_WIDTH), const),
            pl.BlockSpec((1, GLA_KEY_WIDTH), const),
        ],
        out_specs=[
            pl.BlockSpec((tm, P_COLS), lambda i: (i, 0)),
            pl.BlockSpec((tm, GLA_KEY_WIDTH), lambda i: (i, 0)),
        ],
        out_shape=[
            jax.ShapeDtypeStruct((T, P_COLS), BF16),
            jax.ShapeDtypeStruct((T, GLA_KEY_WIDTH), F32),
        ],
        compiler_params=pltpu.CompilerParams(
            dimension_semantics=("arbitrary",), vmem_limit_bytes=VMEM_LIMIT_BYTES),
        name="inproj",
    )(x2d, g, w_main, w_lr, w_gate, b_gate)


def _sb_constants(tile):
    row = jnp.arange(4 * tile)[:, None] % (2 * tile)
    col = jnp.arange(2 * tile)[None, :]
    later2 = ((row > col) & ((row >= tile) == (col >= tile))).astype(BF16)
    t_idx = jnp.arange(tile)[:, None]
    s_idx = jnp.arange(2 * tile)[None, :] % tile
    not_causal = jnp.where(s_idx < t_idx, -SB_MASKED_SCORE * 1e30, SB_MASKED_SCORE).astype(F32)
    r = jnp.arange(2 * LANES)[:, None] % LANES
    c = jnp.arange(LANES)[None, :]
    head_ones = ((r // SB_HEAD_DIM) == (c // SB_HEAD_DIM)).astype(BF16)
    return later2, not_causal, head_ones


def _sb_kernel(q_ref, k_ref, v_ref, g_ref, later2_ref, not_causal_ref, head_ones_ref, o_ref,
               kb_ref, vb_ref, acc_ref, carry_ref, live_ref, *, tile, n_sub):
    assert n_sub >= 2
    n_pairs = SB_HEADS // 2
    pairs = range(n_pairs)
    i = pl.program_id(1)

    @pl.when(i == 0)
    def _():
        low = lax.broadcasted_iota(jnp.int32, (1, SB_WIDTH), 1) % LANES < SB_HEAD_DIM
        for src, dst in ((k_ref, kb_ref), (v_ref, vb_ref)):
            dst[0] = jnp.where(low, src[...], 0)
            dst[1] = jnp.where(low, 0, src[...])

    def both_heads(ref, keys, p):
        cols = slice(p * LANES, (p + 1) * LANES)
        return jnp.concatenate([ref[0, keys, cols], ref[1, keys, cols]], axis=0)

    def window(jobs, diagonal_first):
        units = [(r, p, b) for r, key_starts in jobs for p in pairs for b in range(len(key_starts))]
        keys = {r: [pl.ds(ks, tile) for ks in key_starts] for r, key_starts in jobs}
        zn, logsig, totals, after = {}, {}, {}, {}
        state = {}

        def scores(u):
            r, p, b = u
            zn[u] = _dot_nt(q_ref[r * tile:(r + 1) * tile, p * LANES:(p + 1) * LANES],
                            both_heads(kb_ref, keys[r][b], p))

        def log_terms(u):
            r, p, b = u
            z = zn.pop(u)
            if diagonal_first and b == 0:
                z = jnp.maximum(z, not_causal_ref[...])
            log1m = jnp.minimum(z, 0.0) - jnp.log2(1.0 + jnp.exp2(-jnp.abs(z)))
            logsig[u] = log1m - z
            totals[u] = (log1m[:, 0:1], log1m[:, tile:tile + 1])
            after[u] = _dot(log1m.astype(BF16), later2_ref[0:2 * tile, :])

        def weights_and_values(u):
            r, p, b = u
            if b == 0:
                state["carry"] = None if diagonal_first else carry_ref[r, p]
                state["pv"] = None
                if p == 0:
                    state["live"] = None
            carry = state["carry"]
            aft = after.pop(u)
            first = totals.pop(u)
            total = jnp.concatenate(
                [jnp.broadcast_to(aft[:, 0:1] + first[0], (tile, tile)),
                 jnp.broadcast_to(aft[:, tile:tile + 1] + first[1], (tile, tile))], axis=1)
            if carry is not None:
                aft = aft + carry
                total = total + carry
            state["carry"] = total
            w = jnp.exp2(logsig.pop(u) + aft)
            pv = _dot(w.astype(BF16), both_heads(vb_ref, keys[r][b], p))
            state["pv"] = pv if state["pv"] is None else state["pv"] + pv
            if b == len(keys[r]) - 1:
                carry_ref[r, p] = total
                state["live"] = total if state["live"] is None else jnp.maximum(state["live"], total)
                if diagonal_first:
                    acc_ref[r, p] = state["pv"]
                else:
                    acc_ref[r, p] += state["pv"]
                if p == n_pairs - 1:
                    live_ref[r] = jnp.max(state["live"])

        lag = 2
        n = len(units)
        for t in range(n + 2 * lag):
            if t < n:
                scores(units[t])
            if 0 <= t - lag < n:
                log_terms(units[t - lag])
            if 0 <= t - 2 * lag < n:
                weights_and_values(units[t - 2 * lag])

    def first_blocks(r, n_blocks):
        g0 = (i * n_sub + r) * tile
        return (r, [g0 - b * tile for b in range(n_blocks)])

    @pl.when(i == 0)
    def _():
        window([first_blocks(r, min(r + 1, 3)) for r in range(n_sub)], True)

    @pl.when(i > 0)
    def _():
        window([first_blocks(r, 3) for r in range(n_sub)], True)

    for r in range(n_sub):
        def cond(j, r=r):
            return jnp.logical_and(j >= 0, live_ref[r] > SB_DEAD_LOG2_WEIGHT)

        def body(j, r=r):
            window([(r, [j * tile])], False)
            return j - 1

        lax.while_loop(cond, body, i * n_sub + r - 3)

    outs = [[acc_ref[r, p] for p in pairs] for r in range(n_sub)]
    sumsq = [[_dot(jnp.concatenate(_split_bf16(o * o), axis=1), head_ones_ref[...]) for o in row] for row in outs]
    for r in range(n_sub):
        for p in pairs:
            inv = lax.rsqrt(sumsq[r][p] * (1.0 / SB_HEAD_DIM) + EPS)
            o_ref[r * tile:(r + 1) * tile, p * LANES:(p + 1) * LANES] = (
                outs[r][p] * inv * g_ref[:, p * LANES:(p + 1) * LANES]).astype(BF16)


def _sb_attention(p_act, g, *, batch, seq, tile, n_sub):
    T = batch * seq
    tq = n_sub * tile
    nq = seq // tq
    n_pairs = SB_HEADS // 2
    later2, not_causal, head_ones = _sb_constants(tile)
    return pl.pallas_call(
        functools.partial(_sb_kernel, tile=tile, n_sub=n_sub),
        grid=(batch, nq),
        in_specs=[
            pl.BlockSpec((tq, SB_WIDTH), lambda b, i: (b * nq + i, P_SB_Q // SB_WIDTH)),
            pl.BlockSpec((seq, SB_WIDTH), lambda b, i: (b, P_SB_K // SB_WIDTH)),
            pl.BlockSpec((seq, SB_WIDTH), lambda b, i: (b, P_SB_V // SB_WIDTH)),
            pl.BlockSpec((1, SB_WIDTH), lambda b, i: (0, 0)),
            pl.BlockSpec(later2.shape, lambda b, i: (0, 0)),
            pl.BlockSpec(not_causal.shape, lambda b, i: (0, 0)),
            pl.BlockSpec(head_ones.shape, lambda b, i: (0, 0)),
        ],
        out_specs=pl.BlockSpec((tq, SB_WIDTH), lambda b, i: (b * nq + i, 0)),
        out_shape=jax.ShapeDtypeStruct((T, SB_WIDTH), BF16),
        scratch_shapes=[
            pltpu.VMEM((2, seq, SB_WIDTH), BF16),
            pltpu.VMEM((2, seq, SB_WIDTH), BF16),
            pltpu.VMEM((n_sub, n_pairs, tile, LANES), F32),
            pltpu.VMEM((n_sub, n_pairs, tile, 2 * tile), F32),
            pltpu.SMEM((n_sub,), F32),
        ],
        compiler_params=pltpu.CompilerParams(
            dimension_semantics=("arbitrary", "arbitrary"), vmem_limit_bytes=VMEM_LIMIT_BYTES),
        name="sb",
    )(p_act, p_act, p_act, g, later2, not_causal, head_ones)


def _gla_kernel(q_ref, k_ref, v_ref, og_ref, la_ref, g_ref, o_ref, state_ref, *, seq, step_rows):
    C = GLA_CHUNK
    R = step_rows
    n_chunks = R // C
    pairs = range(GLA_HEADS // 2)
    low_half = lax.broadcasted_iota(jnp.int32, (1, LANES), 1) < GLA_DK
    ri = lax.broadcasted_iota(jnp.int32, (R, R), 0)
    ci = lax.broadcasted_iota(jnp.int32, (R, R), 1)
    same_chunk = (ri // C) == (ci // C)
    causal = same_chunk & (ci <= ri)
    prefix = jnp.where(causal, 1.0, 0.0).astype(BF16)
    prefix2 = jnp.concatenate([prefix, prefix], axis=1)
    state_ref[...] = jnp.zeros_like(state_ref)

    def front(s, out):
        rows = pl.ds(pl.multiple_of(s * R, R), R)
        la_hi, la_lo = _split_bf16(la_ref[rows, :])
        b = _dot(prefix2, jnp.concatenate([la_hi, la_lo], axis=0))
        yield
        b_last = [b[c * C + C - 1:(c + 1) * C, :] for c in range(n_chunks)]
        b_end = jnp.concatenate([jnp.broadcast_to(r, (C, r.shape[1])) for r in b_last], axis=0)
        q = q_ref[rows, :].astype(F32) * (GLA_DK ** -0.5)
        k = k_ref[rows, :].astype(F32)
        q_dec = q * jnp.exp(b)
        k_inv = (k * jnp.exp(-b)).astype(BF16)
        yield
        k_end = k * jnp.exp(b_end - b)
        decay = [jnp.exp(r) for r in b_last]
        q_half, k_end_half = [], []
        for p in pairs:
            ks = slice(p * LANES, (p + 1) * LANES)
            q_half.append((jnp.where(low_half, q_dec[:, ks], 0.0).astype(BF16),
                           jnp.where(low_half, 0.0, q_dec[:, ks]).astype(BF16)))
            k_end_half.append((jnp.where(low_half, k_end[:, ks], 0.0).astype(BF16),
                               jnp.where(low_half, 0.0, k_end[:, ks]).astype(BF16)))
        yield
        v = [v_ref[rows, h * GLA_DV:(h + 1) * GLA_DV] for h in range(GLA_HEADS)]
        attn, o_intra = [], []
        for h in range(GLA_HEADS):
            scores = _dot_nt(q_half[h // 2][h % 2], k_inv[:, (h // 2) * LANES:(h // 2 + 1) * LANES])
            attn.append(jnp.where(causal, scores, 0.0).astype(BF16))
            if h % 2 == 1:
                yield
        for h in range(GLA_HEADS):
            o_intra.append(_dot(attn[h], v[h]))
        yield
        kv = []
        for p in pairs:
            kv.append([_dot_tn(jnp.concatenate([v[2 * p][c * C:(c + 1) * C], v[2 * p + 1][c * C:(c + 1) * C]], axis=0),
                               jnp.concatenate([k_end_half[p][0][c * C:(c + 1) * C],
                                                k_end_half[p][1][c * C:(c + 1) * C]], axis=0))
                       for c in range(n_chunks)])
            yield
        out.extend([rows, q_half, decay, o_intra, kv])

    def back(rows, q_half, decay, o_intra, kv):
        o_inter = [[] for _ in range(GLA_HEADS)]
        for p in pairs:
            state_t = state_ref[p]
            for c in range(n_chunks):
                state_b = state_t.astype(BF16)
                for hh in range(2):
                    o_inter[2 * p + hh].append(_dot_nt(q_half[p][hh][c * C:(c + 1) * C], state_b))
                state_t = state_t * decay[c][:, p * LANES:(p + 1) * LANES] + kv[p][c]
            state_ref[p] = state_t
            yield
        for h in range(GLA_HEADS):
            vs = slice(h * GLA_DV, (h + 1) * GLA_DV)
            o = o_intra[h] + jnp.concatenate(o_inter[h], axis=0)
            inv = lax.rsqrt(jnp.mean(o * o, axis=-1, keepdims=True) + EPS)
            gate = og_ref[rows, vs].astype(F32)
            gate = gate / (1.0 + jnp.exp(-gate))
            o_ref[rows, vs] = (o * inv * g_ref[:, vs] * gate).astype(BF16)
            yield

    def interleave(*gens):
        live = list(gens)
        while live:
            for gen in list(live):
                try:
                    next(gen)
                except StopIteration:
                    live.remove(gen)

    def group(gi, carry):
        s0 = gi * steps_per_group
        pending = []
        interleave(front(s0, pending))
        for d in range(1, steps_per_group):
            ahead = []
            interleave(front(s0 + d, ahead), back(*pending))
            pending = ahead
        interleave(back(*pending))
        return carry

    n_steps = seq // R
    steps_per_group = 8 if n_steps % 8 == 0 else 1
    lax.fori_loop(0, n_steps // steps_per_group, group, 0)


def _gla(p_act, log_a, g, *, batch, seq):
    T = batch * seq
    n_pairs = GLA_HEADS // 2
    return pl.pallas_call(
        functools.partial(_gla_kernel, seq=seq, step_rows=min(256, seq)),
        grid=(batch,),
        in_specs=[
            pl.BlockSpec((seq, GLA_KEY_WIDTH), lambda b: (b, P_G_Q // GLA_KEY_WIDTH)),
            pl.BlockSpec((seq, GLA_KEY_WIDTH), lambda b: (b, P_G_K // GLA_KEY_WIDTH)),
            pl.BlockSpec((seq, GLA_WIDTH), lambda b: (b, P_G_V // GLA_WIDTH)),
            pl.BlockSpec((seq, GLA_WIDTH), lambda b: (b, P_G_OG // GLA_WIDTH)),
            pl.BlockSpec((seq, GLA_KEY_WIDTH), lambda b: (b, 0)),
            pl.BlockSpec((1, GLA_WIDTH), lambda b: (0, 0)),
        ],
        out_specs=pl.BlockSpec((seq, GLA_WIDTH), lambda b: (b, 0)),
        out_shape=jax.ShapeDtypeStruct((T, GLA_WIDTH), BF16),
        scratch_shapes=[pltpu.VMEM((n_pairs, GLA_DV, LANES), F32)],
        compiler_params=pltpu.CompilerParams(
            dimension_semantics=("arbitrary",), vmem_limit_bytes=VMEM_LIMIT_BYTES),
        name="gla",
    )(p_act, p_act, p_act, p_act, log_a, g)


def _rms(x, g):
    ms = jnp.mean(x * x, axis=-1, keepdims=True)
    return x * lax.rsqrt(ms + EPS) * g


def _ffn_kernel(x_ref, osb_ref, ogla_ref, wo_sb_ref, wo_gla_ref, g2_ref, wup_ref, cw_ref, cb_ref,
                wdn_ref, gf_ref, out_ref, x1_ref, h2_ref, acc_ref, u00_ref, u01_ref, u10_ref, u11_ref,
                halo_ref,
                *, tm, fc, tiles_per_seq):
    t = pl.program_id(0)
    n_chunks = D_FF // fc
    H = SUBLANES

    @pl.when(t % tiles_per_seq == 0)
    def _():
        halo_ref[...] = jnp.zeros_like(halo_ref)

    x1 = x_ref[...] + _dot(osb_ref[...], wo_sb_ref[...]) + _dot(ogla_ref[...], wo_gla_ref[...])
    x1_ref[...] = x1
    h2_ref[...] = _rms(x1, g2_ref[...]).astype(BF16)

    u_bufs = ((u00_ref, u01_ref), (u10_ref, u11_ref))

    def cols_of(part, c):
        return slice(part * D_FF + c * fc, part * D_FF + (c + 1) * fc)

    def up_project(c, part):
        buf = u_bufs[c % 2][part]
        u = _dot(h2_ref[...], wup_ref[:, cols_of(part, c)])
        buf[0:H, :] = halo_ref[part, c]
        buf[H:H + tm, :] = u
        halo_ref[part, c] = u[tm - H:tm, :]

    def conv_gate_down(c, r0, nrows):
        conv = []
        for part in range(2):
            buf = u_bufs[c % 2][part]
            cw = cw_ref[:, cols_of(part, c)]
            win = buf[r0:H + r0 + nrows, :]
            conv.append(cb_ref[:, cols_of(part, c)] + cw[2:3, :] * win[H:, :]
                        + cw[1:2, :] * pltpu.roll(win, 1, axis=0)[H:, :]
                        + cw[0:1, :] * pltpu.roll(win, 2, axis=0)[H:, :])
        a, val = conv
        half_a = 0.5 * a
        gated = ((half_a + half_a * jnp.tanh(half_a)) * val).astype(BF16)
        down = _dot(gated, wdn_ref[c * fc:(c + 1) * fc, :])
        if c == 0:
            acc_ref[r0:r0 + nrows, :] = down
        else:
            acc_ref[r0:r0 + nrows, :] += down

    half = tm // 2
    up_project(0, 0)
    up_project(0, 1)
    for c in range(n_chunks):
        for part in range(2):
            if c + 1 < n_chunks:
                up_project(c + 1, part)
            conv_gate_down(c, part * half, half)

    out_ref[...] = _rms(x1_ref[...] + acc_ref[...], gf_ref[...])


def _ffn(x2d, o_sb, o_gla, wo_sb, wo_gla, g2, w_up, conv_w, conv_b, w_down, gf, *, tm, fc, seq):
    T = x2d.shape[0]
    n_chunks = D_FF // fc
    const = lambda i: (0, 0)
    resident = functools.partial(pl.BlockSpec, index_map=const, pipeline_mode=pl.Buffered(1))
    return pl.pallas_call(
        functools.partial(_ffn_kernel, tm=tm, fc=fc, tiles_per_seq=seq // tm),
        grid=(T // tm,),
        in_specs=[
            pl.BlockSpec((tm, D_MODEL), lambda i: (i, 0)),
            pl.BlockSpec((tm, SB_WIDTH), lambda i: (i, 0)),
            pl.BlockSpec((tm, GLA_WIDTH), lambda i: (i, 0)),
            resident((SB_WIDTH, D_MODEL)),
            resident((GLA_WIDTH, D_MODEL)),
            resident((1, D_MODEL)),
            resident((D_MODEL, 2 * D_FF)),
            resident((CONV_WIDTH, 2 * D_FF)),
            resident((1, 2 * D_FF)),
            resident((D_FF, D_MODEL)),
            resident((1, D_MODEL)),
        ],
        out_specs=pl.BlockSpec((tm, D_MODEL), lambda i: (i, 0)),
        out_shape=jax.ShapeDtypeStruct((T, D_MODEL), F32),
        scratch_shapes=[
            pltpu.VMEM((tm, D_MODEL), F32),
            pltpu.VMEM((tm, D_MODEL), BF16),
            pltpu.VMEM((tm, D_MODEL), F32),
            pltpu.VMEM((SUBLANES + tm, fc), F32),
            pltpu.VMEM((SUBLANES + tm, fc), F32),
            pltpu.VMEM((SUBLANES + tm, fc), F32),
            pltpu.VMEM((SUBLANES + tm, fc), F32),
            pltpu.VMEM((2, n_chunks, SUBLANES, fc), F32),
        ],
        compiler_params=pltpu.CompilerParams(
            dimension_semantics=("arbitrary",), vmem_limit_bytes=VMEM_LIMIT_BYTES),
        name="ffn",
    )(x2d, o_sb, o_gla, wo_sb, wo_gla, g2, w_up, conv_w, conv_b, w_down, gf)


def kernel(x, attn_norm_g, w_in, w_gate_up, b_gate_up, sb_out_g, gla_out_g, w_out, ffn_norm_g,
           w_ffn_up, conv_w, conv_b, w_ffn_down, final_norm_g):
    batch, seq, d_model = x.shape
    assert d_model == D_MODEL and attn_norm_g.shape[0] == 1
    T = batch * seq
    x2d = x.reshape(T, D_MODEL)

    lr0 = P_G_OG
    w_in0 = w_in[0]
    sb_q_scale = -(SB_HEAD_DIM ** -0.5) * LOG2_E
    w_main = jnp.concatenate([w_in0[:, :SB_WIDTH] * sb_q_scale, w_in0[:, SB_WIDTH:lr0],
                              w_in0[:, lr0 + GLA_GATE_RANK:]], axis=1).astype(BF16)
    w_lr = jnp.pad(w_in0[:, lr0:lr0 + GLA_GATE_RANK], ((0, 0), (0, LANES - GLA_GATE_RANK))).astype(BF16)
    w_gate = jnp.pad(w_gate_up[0], ((0, LANES - GLA_GATE_RANK), (0, 0))).astype(BF16)

    tm = min(512, seq)
    p_act, log_a = _inproj(x2d, attn_norm_g, w_main, w_lr, w_gate, b_gate_up, tm=tm)
    o_sb = _sb_attention(p_act, sb_out_g, batch=batch, seq=seq, tile=LANES, n_sub=8)
    o_gla = _gla(p_act, log_a, gla_out_g, batch=batch, seq=seq)
    out = _ffn(x2d, o_sb, o_gla,
               w_out[0, :SB_WIDTH].astype(BF16), w_out[0, SB_WIDTH:].astype(BF16),
               ffn_norm_g, w_ffn_up[0].astype(BF16), conv_w[0], conv_b, w_ffn_down[0].astype(BF16),
               final_norm_g.reshape(1, D_MODEL), tm=tm, fc=256, seq=seq)
    return out.reshape(batch, seq, D_MODEL)
```

```python
import functools

import jax
import jax.numpy as jnp
from jax import lax
from jax.experimental import pallas as pl
from jax.experimental.pallas import tpu as pltpu

F32 = jnp.float32
BF16 = jnp.bfloat16

D_MODEL = 1024
SB_HEAD_DIM = 64
SB_WIDTH = 512
SB_HEADS = SB_WIDTH // SB_HEAD_DIM
GLA_HEADS = 4
GLA_WIDTH = 512
GLA_DV = GLA_WIDTH // GLA_HEADS
GLA_DK = GLA_DV // 2
GLA_KEY_WIDTH = GLA_HEADS * GLA_DK
GLA_GATE_RANK = 16
GLA_GATE_NORMALIZER = 16.0
GLA_CHUNK = 64
D_FF = 2816
CONV_WIDTH = 3
EPS = 1e-6

LANES = 128
SUBLANES = 8
VMEM_LIMIT_BYTES = 56 * 1024 * 1024

P_SB_Q, P_SB_K, P_SB_V = 0, 512, 1024
P_G_Q, P_G_K, P_G_V, P_G_OG = 1536, 1792, 2048, 2560
P_COLS = 3072

LOG2_E = 1.4426950408889634
SB_DEAD_LOG2_WEIGHT = -127.0
SB_MASKED_SCORE = 1.0e4


def _dot(a, b):
    return jnp.dot(a, b, preferred_element_type=F32)


def _dot_nt(a, b):
    return lax.dot_general(a, b, (((1,), (1,)), ((), ())), preferred_element_type=F32)


def _dot_tn(a, b):
    return lax.dot_general(a, b, (((0,), (0,)), ((), ())), preferred_element_type=F32)


def _split_bf16(x):
    hi = x.astype(BF16)
    lo = (x - hi.astype(F32)).astype(BF16)
    return hi, lo


def _log_sigmoid(x):
    return jnp.minimum(x, 0.0) - jnp.log(1.0 + jnp.exp(-jnp.abs(x)))


def _inproj_kernel(x_ref, g_ref, w_ref, wlr_ref, wg_ref, bg_ref, p_ref, la_ref, *, col_chunk):
    x = x_ref[...]
    ms = jnp.mean(x * x, axis=-1, keepdims=True)
    y = (x * lax.rsqrt(ms + EPS) * g_ref[...]).astype(BF16)
    for c in range(0, P_COLS, col_chunk):
        p_ref[:, c:c + col_chunk] = _dot(y, w_ref[:, c:c + col_chunk]).astype(BF16)
    g_lr = _dot(y, wlr_ref[...])
    logit = _dot(g_lr.astype(BF16), wg_ref[...]) + bg_ref[...]
    la_ref[...] = _log_sigmoid(logit) * (1.0 / GLA_GATE_NORMALIZER)


def _inproj(x2d, g, w_main, w_lr, w_gate, b_gate, *, tm):
    T = x2d.shape[0]
    const = lambda i: (0, 0)
    return pl.pallas_call(
        functools.partial(_inproj_kernel, col_chunk=512),
        grid=(T // tm,),
        in_specs=[
            pl.BlockSpec((tm, D_MODEL), lambda i: (i, 0)),
            pl.BlockSpec((1, D_MODEL), const),
            pl.BlockSpec((D_MODEL, P_COLS), const),
            pl.BlockSpec((D_MODEL, LANES), const),
            pl.BlockSpec((LANES, GLA_KEY_WIDTH), const),
            pl.BlockSpec((1, GLA_KEY_WIDTH), const),
        ],
        out_specs=[
            pl.BlockSpec((tm, P_COLS), lambda i: (i, 0)),
            pl.BlockSpec((tm, GLA_KEY_WIDTH), lambda i: (i, 0)),
        ],
        out_shape=[
            jax.ShapeDtypeStruct((T, P_COLS), BF16),
            jax.ShapeDtypeStruct((T, GLA_KEY_WIDTH), F32),
        ],
        compiler_params=pltpu.CompilerParams(
            dimension_semantics=("arbitrary",), vmem_limit_bytes=VMEM_LIMIT_BYTES),
        name="inproj",
    )(x2d, g, w_main, w_lr, w_gate, b_gate)


def _sb_constants(tile):
    row = jnp.arange(4 * tile)[:, None] % (2 * tile)
    col = jnp.arange(2 * tile)[None, :]
    later2 = ((row > col) & ((row >= tile) == (col >= tile))).astype(BF16)
    t_idx = jnp.arange(tile)[:, None]
    s_idx = jnp.arange(2 * tile)[None, :] % tile
    not_causal = jnp.where(s_idx < t_idx, -SB_MASKED_SCORE * 1e30, SB_MASKED_SCORE).astype(F32)
    r = jnp.arange(2 * LANES)[:, None] % LANES
    c = jnp.arange(LANES)[None, :]
    head_ones = ((r // SB_HEAD_DIM) == (c // SB_HEAD_DIM)).astype(BF16)
    return later2, not_causal, head_ones


def _sb_kernel(q_ref, k_ref, v_ref, g_ref, later2_ref, not_causal_ref, head_ones_ref, o_ref,
               kb_ref, vb_ref, acc_ref, carry_ref, live_ref, *, tile, n_sub):
    assert n_sub >= 2
    n_pairs = SB_HEADS // 2
    pairs = range(n_pairs)
    i = pl.program_id(1)

    @pl.when(i == 0)
    def _():
        low = lax.broadcasted_iota(jnp.int32, (1, SB_WIDTH), 1) % LANES < SB_HEAD_DIM
        for src, dst in ((k_ref, kb_ref), (v_ref, vb_ref)):
            dst[0] = jnp.where(low, src[...], 0)
            dst[1] = jnp.where(low, 0, src[...])

    def both_heads(ref, keys, p):
        cols = slice(p * LANES, (p + 1) * LANES)
        return jnp.concatenate([ref[0, keys, cols], ref[1, keys, cols]], axis=0)

    def window(jobs, diagonal_first):
        units = [(r, p, b) for r, key_starts in jobs for p in pairs for b in range(len(key_starts))]
        keys = {r: [pl.ds(ks, tile) for ks in key_starts] for r, key_starts in jobs}
        zn, logsig, totals, after = {}, {}, {}, {}
        state = {}

        def scores(u):
            r, p, b = u
            zn[u] = _dot_nt(q_ref[r * tile:(r + 1) * tile, p * LANES:(p + 1) * LANES],
                            both_heads(kb_ref, keys[r][b], p))

        def log_terms(u):
            r, p, b = u
            z = zn.pop(u)
            if diagonal_first and b == 0:
                z = jnp.maximum(z, not_causal_ref[...])
            log1m = jnp.minimum(z, 0.0) - jnp.log2(1.0 + jnp.exp2(-jnp.abs(z)))
            logsig[u] = log1m - z
            totals[u] = (log1m[:, 0:1], log1m[:, tile:tile + 1])
            after[u] = _dot(log1m.astype(BF16), later2_ref[0:2 * tile, :])

        def weights_and_values(u):
            r, p, b = u
            if b == 0:
                state["carry"] = None if diagonal_first else carry_ref[r, p]
                state["pv"] = None
                if p == 0:
                    state["live"] = None
            carry = state["carry"]
            aft = after.pop(u)
            first = totals.pop(u)
            total = jnp.concatenate(
                [jnp.broadcast_to(aft[:, 0:1] + first[0], (tile, tile)),
                 jnp.broadcast_to(aft[:, tile:tile + 1] + first[1], (tile, tile))], axis=1)
            if carry is not None:
                aft = aft + carry
                total = total + carry
            state["carry"] = total
            w = jnp.exp2(logsig.pop(u) + aft)
            pv = _dot(w.astype(BF16), both_heads(vb_ref, keys[r][b], p))
            state["pv"] = pv if state["pv"] is None else state["pv"] + pv
            if b == len(keys[r]) - 1:
                carry_ref[r, p] = total
                state["live"] = total if state["live"] is None else jnp.maximum(state["live"], total)
                if diagonal_first:
                    acc_ref[r, p] = state["pv"]
                else:
                    acc_ref[r, p] += state["pv"]
                if p == n_pairs - 1:
                    live_ref[r] = jnp.max(state["live"])

        lag = 2
        n = len(units)
        for t in range(n + 2 * lag):
            if t < n:
                scores(units[t])
            if 0 <= t - lag < n:
                log_terms(units[t - lag])
            if 0 <= t - 2 * lag < n:
                weights_and_values(units[t - 2 * lag])

    def first_blocks(r, n_blocks):
        g0 = (i * n_sub + r) * tile
        return (r, [g0 - b * tile for b in range(n_blocks)])

    @pl.when(i == 0)
    def _():
        window([first_blocks(r, min(r + 1, 3)) for r in range(n_sub)], True)

    @pl.when(i > 0)
    def _():
        window([first_blocks(r, 3) for r in range(n_sub)], True)

    for r in range(n_sub):
        def cond(j, r=r):
            return jnp.logical_and(j >= 0, live_ref[r] > SB_DEAD_LOG2_WEIGHT)

        def body(j, r=r):
            window([(r, [j * tile])], False)
            return j - 1

        lax.while_loop(cond, body, i * n_sub + r - 3)

    outs = [[acc_ref[r, p] for p in pairs] for r in range(n_sub)]
    sumsq = [[_dot(jnp.concatenate(_split_bf16(o * o), axis=1), head_ones_ref[...]) for o in row] for row in outs]
    for r in range(n_sub):
        for p in pairs:
            inv = lax.rsqrt(sumsq[r][p] * (1.0 / SB_HEAD_DIM) + EPS)
            o_ref[r * tile:(r + 1) * tile, p * LANES:(p + 1) * LANES] = (
                outs[r][p] * inv * g_ref[:, p * LANES:(p + 1) * LANES]).astype(BF16)


def _sb_attention(p_act, g, *, batch, seq, tile, n_sub):
    T = batch * seq
    tq = n_sub * tile
    nq = seq // tq
    n_pairs = SB_HEADS // 2
    later2, not_causal, head_ones = _sb_constants(tile)
    return pl.pallas_call(
        functools.partial(_sb_kernel, tile=tile, n_sub=n_sub),
        grid=(batch, nq),
        in_specs=[
            pl.BlockSpec((tq, SB_WIDTH), lambda b, i: (b * nq + i, P_SB_Q // SB_WIDTH)),
            pl.BlockSpec((seq, SB_WIDTH), lambda b, i: (b, P_SB_K // SB_WIDTH)),
            pl.BlockSpec((seq, SB_WIDTH), lambda b, i: (b, P_SB_V // SB_WIDTH)),
            pl.BlockSpec((1, SB_WIDTH), lambda b, i: (0, 0)),
            pl.BlockSpec(later2.shape, lambda b, i: (0, 0)),
            pl.BlockSpec(not_causal.shape, lambda b, i: (0, 0)),
            pl.BlockSpec(head_ones.shape, lambda b, i: (0, 0)),
        ],
        out_specs=pl.BlockSpec((tq, SB_WIDTH), lambda b, i: (b * nq + i, 0)),
        out_shape=jax.ShapeDtypeStruct((T, SB_WIDTH), BF16),
        scratch_shapes=[
            pltpu.VMEM((2, seq, SB_WIDTH), BF16),
            pltpu.VMEM((2, seq, SB_WIDTH), BF16),
            pltpu.VMEM((n_sub, n_pairs, tile, LANES), F32),
            pltpu.VMEM((n_sub, n_pairs, tile, 2 * tile), F32),
            pltpu.SMEM((n_sub,), F32),
        ],
        compiler_params=pltpu.CompilerParams(
            dimension_semantics=("arbitrary", "arbitrary"), vmem_limit_bytes=VMEM_LIMIT_BYTES),
        name="sb",
    )(p_act, p_act, p_act, g, later2, not_causal, head_ones)


def _gla_kernel(q_ref, k_ref, v_ref, og_ref, la_ref, g_ref, o_ref, state_ref, *, seq, step_rows):
    C = GLA_CHUNK
    R = step_rows
    n_chunks = R // C
    pairs = range(GLA_HEADS // 2)
    low_half = lax.broadcasted_iota(jnp.int32, (1, LANES), 1) < GLA_DK
    ri = lax.broadcasted_iota(jnp.int32, (R, R), 0)
    ci = lax.broadcasted_iota(jnp.int32, (R, R), 1)
    same_chunk = (ri // C) == (ci // C)
    causal = same_chunk & (ci <= ri)
    prefix = jnp.where(causal, 1.0, 0.0).astype(BF16)
    prefix2 = jnp.concatenate([prefix, prefix], axis=1)
    state_ref[...] = jnp.zeros_like(state_ref)

    def front(s, out):
        rows = pl.ds(pl.multiple_of(s * R, R), R)
        la_hi, la_lo = _split_bf16(la_ref[rows, :])
        b = _dot(prefix2, jnp.concatenate([la_hi, la_lo], axis=0))
        yield
        b_last = [b[c * C + C - 1:(c + 1) * C, :] for c in range(n_chunks)]
        b_end = jnp.concatenate([jnp.broadcast_to(r, (C, r.shape[1])) for r in b_last], axis=0)
        q = q_ref[rows, :].astype(F32) * (GLA_DK ** -0.5)
        k = k_ref[rows, :].astype(F32)
        q_dec = q * jnp.exp(b)
        k_inv = (k * jnp.exp(-b)).astype(BF16)
        yield
        k_end = k * jnp.exp(b_end - b)
        decay = [jnp.exp(r) for r in b_last]
        q_half, k_end_half = [], []
        for p in pairs:
            ks = slice(p * LANES, (p + 1) * LANES)
            q_half.append((jnp.where(low_half, q_dec[:, ks], 0.0).astype(BF16),
                           jnp.where(low_half, 0.0, q_dec[:, ks]).astype(BF16)))
            k_end_half.append((jnp.where(low_half, k_end[:, ks], 0.0).astype(BF16),
                               jnp.where(low_half, 0.0, k_end[:, ks]).astype(BF16)))
        yield
        v = [v_ref[rows, h * GLA_DV:(h + 1) * GLA_DV] for h in range(GLA_HEADS)]
        attn, o_intra = [], []
        for h in range(GLA_HEADS):
            scores = _dot_nt(q_half[h // 2][h % 2], k_inv[:, (h // 2) * LANES:(h // 2 + 1) * LANES])
            attn.append(jnp.where(causal, scores, 0.0).astype(BF16))
            if h % 2 == 1:
                yield
        for h in range(GLA_HEADS):
            o_intra.append(_dot(attn[h], v[h]))
        yield
        kv = []
        for p in pairs:
            kv.append([_dot_tn(jnp.concatenate([v[2 * p][c * C:(c + 1) * C], v[2 * p + 1][c * C:(c + 1) * C]], axis=0),
                               jnp.concatenate([k_end_half[p][0][c * C:(c + 1) * C],
                                                k_end_half[p][1][c * C:(c + 1) * C]], axis=0))
                       for c in range(n_chunks)])
            yield
        out.extend([rows, q_half, decay, o_intra, kv])

    def back(rows, q_half, decay, o_intra, kv):
        o_inter = [[] for _ in range(GLA_HEADS)]
        for p in pairs:
            state_t = state_ref[p]
            for c in range(n_chunks):
                state_b = state_t.astype(BF16)
                for hh in range(2):
                    o_inter[2 * p + hh].append(_dot_nt(q_half[p][hh][c * C:(c + 1) * C], state_b))
                state_t = state_t * decay[c][:, p * LANES:(p + 1) * LANES] + kv[p][c]
            state_ref[p] = state_t
            yield
        for h in range(GLA_HEADS):
            vs = slice(h * GLA_DV, (h + 1) * GLA_DV)
            o = o_intra[h] + jnp.concatenate(o_inter[h], axis=0)
            inv = lax.rsqrt(jnp.mean(o * o, axis=-1, keepdims=True) + EPS)
            gate = og_ref[rows, vs].astype(F32)
            gate = gate / (1.0 + jnp.exp(-gate))
            o_ref[rows, vs] = (o * inv * g_ref[:, vs] * gate).astype(BF16)
            yield

    def interleave(*gens):
        live = list(gens)
        while live:
            for gen in list(live):
                try:
                    next(gen)
                except StopIteration:
                    live.remove(gen)

    def group(gi, carry):
        s0 = gi * steps_per_group
        pending = []
        interleave(front(s0, pending))
        for d in range(1, steps_per_group):
            ahead = []
            interleave(front(s0 + d, ahead), back(*pending))
            pending = ahead
        interleave(back(*pending))
        return carry

    n_steps = seq // R
    steps_per_group = 8 if n_steps % 8 == 0 else 1
    lax.fori_loop(0, n_steps // steps_per_group, group, 0)


def _gla(p_act, log_a, g, *, batch, seq):
    T = batch * seq
    n_pairs = GLA_HEADS // 2
    return pl.pallas_call(
        functools.partial(_gla_kernel, seq=seq, step_rows=min(256, seq)),
        grid=(batch,),
        in_specs=[
            pl.BlockSpec((seq, GLA_KEY_WIDTH), lambda b: (b, P_G_Q // GLA_KEY_WIDTH)),
            pl.BlockSpec((seq, GLA_KEY_WIDTH), lambda b: (b, P_G_K // GLA_KEY_WIDTH)),
            pl.BlockSpec((seq, GLA_WIDTH), lambda b: (b, P_G_V // GLA_WIDTH)),
            pl.BlockSpec((seq, GLA_WIDTH), lambda b: (b, P_G_OG // GLA_WIDTH)),
            pl.BlockSpec((seq, GLA_KEY_WIDTH), lambda b: (b, 0)),
            pl.BlockSpec((1, GLA_WIDTH), lambda b: (0, 0)),
        ],
        out_specs=pl.BlockSpec((seq, GLA_WIDTH), lambda b: (b, 0)),
        out_shape=jax.ShapeDtypeStruct((T, GLA_WIDTH), BF16),
        scratch_shapes=[pltpu.VMEM((n_pairs, GLA_DV, LANES), F32)],
        compiler_params=pltpu.CompilerParams(
            dimension_semantics=("arbitrary",), vmem_limit_bytes=VMEM_LIMIT_BYTES),
        name="gla",
    )(p_act, p_act, p_act, p_act, log_a, g)


def _rms(x, g):
    ms = jnp.mean(x * x, axis=-1, keepdims=True)
    return x * lax.rsqrt(ms + EPS) * g


def _ffn_kernel(x_ref, osb_ref, ogla_ref, wo_sb_ref, wo_gla_ref, g2_ref, wup_ref, cw_ref, cb_ref,
                wdn_ref, gf_ref, out_ref, x1_ref, h2_ref, acc_ref, u00_ref, u01_ref, u10_ref, u11_ref,
                halo_ref,
                *, tm, fc, tiles_per_step, steps_per_seq):
    t = pl.program_id(0)
    n_chunks = D_FF // fc
    H = SUBLANES
    n_pieces = 4
    u_bufs = ((u00_ref, u01_ref), (u10_ref, u11_ref))

    @pl.when(t % steps_per_seq == 0)
    def _():
        halo_ref[...] = jnp.zeros_like(halo_ref)

    def cols_of(part, c):
        return slice(part * D_FF + c * fc, part * D_FF + (c + 1) * fc)

    def start(j):
        rows = slice(j * tm, (j + 1) * tm)
        piece = D_MODEL // n_pieces
        for k in range(n_pieces):
            cols = slice(k * piece, (k + 1) * piece)
            x1_ref[j, :, cols] = (x_ref[rows, cols] + _dot(osb_ref[rows, :], wo_sb_ref[:, cols])
                                  + _dot(ogla_ref[rows, :], wo_gla_ref[:, cols]))
            yield
        h2_ref[...] = _rms(x1_ref[j], g2_ref[...]).astype(BF16)
        yield

    def finish(j):
        piece = tm // n_pieces
        for k in range(n_pieces):
            rows = slice(k * piece, (k + 1) * piece)
            out_ref[j * tm + k * piece:j * tm + (k + 1) * piece, :] = _rms(
                x1_ref[j, rows, :] + acc_ref[j, rows, :], gf_ref[...])
            yield

    def up_project(c, part):
        buf = u_bufs[c % 2][part]
        u = _dot(h2_ref[...], wup_ref[:, cols_of(part, c)])
        buf[0:H, :] = halo_ref[part, c]
        buf[H:H + tm, :] = u
        halo_ref[part, c] = u[tm - H:tm, :]

    def conv_gate_down(j, c, r0, nrows):
        conv = []
        for part in range(2):
            buf = u_bufs[c % 2][part]
            cw = cw_ref[:, cols_of(part, c)]
            win = buf[r0:H + r0 + nrows, :]
            conv.append(cb_ref[:, cols_of(part, c)] + cw[2:3, :] * win[H:, :]
                        + cw[1:2, :] * pltpu.roll(win, 1, axis=0)[H:, :]
                        + cw[0:1, :] * pltpu.roll(win, 2, axis=0)[H:, :])
        a, val = conv
        half_a = 0.5 * a
        gated = ((half_a + half_a * jnp.tanh(half_a)) * val).astype(BF16)
        down = _dot(gated, wdn_ref[c * fc:(c + 1) * fc, :])
        if c == 0:
            acc_ref[j, r0:r0 + nrows, :] = down
        else:
            acc_ref[j, r0:r0 + nrows, :] += down

    def conv_ffn(j):
        half = tm // 2
        up_project(0, 0)
        up_project(0, 1)
        for c in range(n_chunks):
            for part in range(2):
                if c + 1 < n_chunks:
                    up_project(c + 1, part)
                conv_gate_down(j, c, part * half, half)

    def interleave(*gens):
        live = list(gens)
        while live:
            for gen in list(live):
                try:
                    next(gen)
                except StopIteration:
                    live.remove(gen)

    interleave(start(0))
    for j in range(tiles_per_step):
        conv_ffn(j)
        if j + 1 < tiles_per_step:
            interleave(start(j + 1), finish(j))
        else:
            interleave(finish(j))


def _ffn(x2d, o_sb, o_gla, wo_sb, wo_gla, g2, w_up, conv_w, conv_b, w_down, gf, *, tm, fc, seq):
    T = x2d.shape[0]
    n_chunks = D_FF // fc
    tiles_per_step = 2 if seq % (2 * tm) == 0 else 1
    rows = tiles_per_step * tm
    const = lambda i: (0, 0)
    resident = functools.partial(pl.BlockSpec, index_map=const, pipeline_mode=pl.Buffered(1))
    return pl.pallas_call(
        functools.partial(_ffn_kernel, tm=tm, fc=fc, tiles_per_step=tiles_per_step,
                          steps_per_seq=seq // rows),
        grid=(T // rows,),
        in_specs=[
            pl.BlockSpec((rows, D_MODEL), lambda i: (i, 0)),
            pl.BlockSpec((rows, SB_WIDTH), lambda i: (i, 0)),
            pl.BlockSpec((rows, GLA_WIDTH), lambda i: (i, 0)),
            resident((SB_WIDTH, D_MODEL)),
            resident((GLA_WIDTH, D_MODEL)),
            resident((1, D_MODEL)),
            resident((D_MODEL, 2 * D_FF)),
            resident((CONV_WIDTH, 2 * D_FF)),
            resident((1, 2 * D_FF)),
            resident((D_FF, D_MODEL)),
            resident((1, D_MODEL)),
        ],
        out_specs=pl.BlockSpec((rows, D_MODEL), lambda i: (i, 0)),
        out_shape=jax.ShapeDtypeStruct((T, D_MODEL), F32),
        scratch_shapes=[
            pltpu.VMEM((tiles_per_step, tm, D_MODEL), F32),
            pltpu.VMEM((tm, D_MODEL), BF16),
            pltpu.VMEM((tiles_per_step, tm, D_MODEL), F32),
            pltpu.VMEM((SUBLANES + tm, fc), F32),
            pltpu.VMEM((SUBLANES + tm, fc), F32),
            pltpu.VMEM((SUBLANES + tm, fc), F32),
            pltpu.VMEM((SUBLANES + tm, fc), F32),
            pltpu.VMEM((2, n_chunks, SUBLANES, fc), F32),
        ],
        compiler_params=pltpu.CompilerParams(
            dimension_semantics=("arbitrary",), vmem_limit_bytes=VMEM_LIMIT_BYTES),
        name="ffn",
    )(x2d, o_sb, o_gla, wo_sb, wo_gla, g2, w_up, conv_w, conv_b, w_down, gf)


def kernel(x, attn_norm_g, w_in, w_gate_up, b_gate_up, sb_out_g, gla_out_g, w_out, ffn_norm_g,
           w_ffn_up, conv_w, conv_b, w_ffn_down, final_norm_g):
    batch, seq, d_model = x.shape
    assert d_model == D_MODEL and attn_norm_g.shape[0] == 1
    T = batch * seq
    x2d = x.reshape(T, D_MODEL)

    lr0 = P_G_OG
    w_in0 = w_in[0]
    sb_q_scale = -(SB_HEAD_DIM ** -0.5) * LOG2_E
    w_main = jnp.concatenate([w_in0[:, :SB_WIDTH] * sb_q_scale, w_in0[:, SB_WIDTH:lr0],
                              w_in0[:, lr0 + GLA_GATE_RANK:]], axis=1).astype(BF16)
    w_lr = jnp.pad(w_in0[:, lr0:lr0 + GLA_GATE_RANK], ((0, 0), (0, LANES - GLA_GATE_RANK))).astype(BF16)
    w_gate = jnp.pad(w_gate_up[0], ((0, LANES - GLA_GATE_RANK), (0, 0))).astype(BF16)

    tm = min(512, seq)
    p_act, log_a = _inproj(x2d, attn_norm_g, w_main, w_lr, w_gate, b_gate_up, tm=tm)
    o_sb = _sb_attention(p_act, sb_out_g, batch=batch, seq=seq, tile=LANES, n_sub=8)
    o_gla = _gla(p_act, log_a, gla_out_g, batch=batch, seq=seq)
    out = _ffn(x2d, o_sb, o_gla,
               w_out[0, :SB_WIDTH].astype(BF16), w_out[0, SB_WIDTH:].astype(BF16),
               ffn_norm_g, w_ffn_up[0].astype(BF16), conv_w[0], conv_b, w_ffn_down[0].astype(BF16),
               final_norm_g.reshape(1, D_MODEL), tm=tm, fc=256, seq=seq)
    return out.reshape(batch, seq, D_MODEL)
```

```python
import functools

import jax
import jax.numpy as jnp
from jax import lax
from jax.experimental import pallas as pl
from jax.experimental.pallas import tpu as pltpu

F32 = jnp.float32
BF16 = jnp.bfloat16

D_MODEL = 1024
SB_HEAD_DIM = 64
SB_WIDTH = 512
SB_HEADS = SB_WIDTH // SB_HEAD_DIM
GLA_HEADS = 4
GLA_WIDTH = 512
GLA_DV = GLA_WIDTH // GLA_HEADS
GLA_DK = GLA_DV // 2
GLA_KEY_WIDTH = GLA_HEADS * GLA_DK
GLA_GATE_RANK = 16
GLA_GATE_NORMALIZER = 16.0
GLA_CHUNK = 64
D_FF = 2816
CONV_WIDTH = 3
EPS = 1e-6

LANES = 128
SUBLANES = 8
VMEM_LIMIT_BYTES = 56 * 1024 * 1024

P_SB_Q, P_SB_K, P_SB_V = 0, 512, 1024
P_G_Q, P_G_K, P_G_V, P_G_OG = 1536, 1792, 2048, 2560
P_COLS = 3072

LOG2_E = 1.4426950408889634
SB_DEAD_LOG2_WEIGHT = -127.0
SB_MASKED_SCORE = 1.0e4


def _dot(a, b):
    return jnp.dot(a, b, preferred_element_type=F32)


def _dot_nt(a, b):
    return lax.dot_general(a, b, (((1,), (1,)), ((), ())), preferred_element_type=F32)


def _dot_tn(a, b):
    return lax.dot_general(a, b, (((0,), (0,)), ((), ())), preferred_element_type=F32)


def _split_bf16(x):
    hi = x.astype(BF16)
    lo = (x - hi.astype(F32)).astype(BF16)
    return hi, lo


def _log_sigmoid(x):
    return jnp.minimum(x, 0.0) - jnp.log(1.0 + jnp.exp(-jnp.abs(x)))


def _inproj_kernel(x_ref, g_ref, w_ref, wlr_ref, p_ref, glr_ref, *, col_chunk):
    x = x_ref[...]
    ms = jnp.mean(x * x, axis=-1, keepdims=True)
    y = (x * lax.rsqrt(ms + EPS) * g_ref[...]).astype(BF16)
    for c in range(0, P_COLS, col_chunk):
        p_ref[:, c:c + col_chunk] = _dot(y, w_ref[:, c:c + col_chunk]).astype(BF16)
    glr_ref[...] = _dot(y, wlr_ref[...]).astype(BF16)


def _inproj(x2d, g, w_main, w_lr, *, tm):
    T = x2d.shape[0]
    const = lambda i: (0, 0)
    return pl.pallas_call(
        functools.partial(_inproj_kernel, col_chunk=512),
        grid=(T // tm,),
        in_specs=[
            pl.BlockSpec((tm, D_MODEL), lambda i: (i, 0)),
            pl.BlockSpec((1, D_MODEL), const),
            pl.BlockSpec((D_MODEL, P_COLS), const),
            pl.BlockSpec((D_MODEL, LANES), const),
        ],
        out_specs=[
            pl.BlockSpec((tm, P_COLS), lambda i: (i, 0)),
            pl.BlockSpec((tm, LANES), lambda i: (i, 0)),
        ],
        out_shape=[
            jax.ShapeDtypeStruct((T, P_COLS), BF16),
            jax.ShapeDtypeStruct((T, LANES), BF16),
        ],
        compiler_params=pltpu.CompilerParams(
            dimension_semantics=("arbitrary",), vmem_limit_bytes=VMEM_LIMIT_BYTES),
        name="inproj",
    )(x2d, g, w_main, w_lr)


def _sb_constants(tile):
    row = jnp.arange(4 * tile)[:, None] % (2 * tile)
    col = jnp.arange(2 * tile)[None, :]
    later2 = ((row > col) & ((row >= tile) == (col >= tile))).astype(BF16)
    t_idx = jnp.arange(tile)[:, None]
    s_idx = jnp.arange(2 * tile)[None, :] % tile
    not_causal = jnp.where(s_idx < t_idx, -SB_MASKED_SCORE * 1e30, SB_MASKED_SCORE).astype(F32)
    r = jnp.arange(2 * LANES)[:, None] % LANES
    c = jnp.arange(LANES)[None, :]
    head_ones = ((r // SB_HEAD_DIM) == (c // SB_HEAD_DIM)).astype(BF16)
    return later2, not_causal, head_ones


def _sb_kernel(q_ref, k_ref, v_ref, g_ref, later2_ref, not_causal_ref, head_ones_ref, o_ref,
               kb_ref, vb_ref, acc_ref, carry_ref, live_ref, *, tile, n_sub):
    assert n_sub >= 2
    n_pairs = SB_HEADS // 2
    pairs = range(n_pairs)
    i = pl.program_id(1)

    @pl.when(i == 0)
    def _():
        low = lax.broadcasted_iota(jnp.int32, (1, SB_WIDTH), 1) % LANES < SB_HEAD_DIM
        for src, dst in ((k_ref, kb_ref), (v_ref, vb_ref)):
            dst[0] = jnp.where(low, src[...], 0)
            dst[1] = jnp.where(low, 0, src[...])

    def both_heads(ref, keys, p):
        cols = slice(p * LANES, (p + 1) * LANES)
        return jnp.concatenate([ref[0, keys, cols], ref[1, keys, cols]], axis=0)

    def window(jobs, diagonal_first):
        units = [(r, p, b) for r, key_starts in jobs for p in pairs for b in range(len(key_starts))]
        keys = {r: [pl.ds(ks, tile) for ks in key_starts] for r, key_starts in jobs}
        zn, logsig, totals, after = {}, {}, {}, {}
        state = {}

        def scores(u):
            r, p, b = u
            zn[u] = _dot_nt(q_ref[r * tile:(r + 1) * tile, p * LANES:(p + 1) * LANES],
                            both_heads(kb_ref, keys[r][b], p))

        def log_terms(u):
            r, p, b = u
            z = zn.pop(u)
            if diagonal_first and b == 0:
                z = jnp.maximum(z, not_causal_ref[...])
            log1m = jnp.minimum(z, 0.0) - jnp.log2(1.0 + jnp.exp2(-jnp.abs(z)))
            logsig[u] = log1m - z
            totals[u] = (log1m[:, 0:1], log1m[:, tile:tile + 1])
            after[u] = _dot(log1m.astype(BF16), later2_ref[0:2 * tile, :])

        def weights_and_values(u):
            r, p, b = u
            if b == 0:
                state["carry"] = None if diagonal_first else carry_ref[r, p]
                state["pv"] = None
                if p == 0:
                    state["live"] = None
            carry = state["carry"]
            aft = after.pop(u)
            first = totals.pop(u)
            total = jnp.concatenate(
                [jnp.broadcast_to(aft[:, 0:1] + first[0], (tile, tile)),
                 jnp.broadcast_to(aft[:, tile:tile + 1] + first[1], (tile, tile))], axis=1)
            if carry is not None:
                aft = aft + carry
                total = total + carry
            state["carry"] = total
            w = jnp.exp2(logsig.pop(u) + aft)
            pv = _dot(w.astype(BF16), both_heads(vb_ref, keys[r][b], p))
            state["pv"] = pv if state["pv"] is None else state["pv"] + pv
            if b == len(keys[r]) - 1:
                carry_ref[r, p] = total
                state["live"] = total if state["live"] is None else jnp.maximum(state["live"], total)
                if diagonal_first:
                    acc_ref[r, p] = state["pv"]
                else:
                    acc_ref[r, p] += state["pv"]
                if p == n_pairs - 1:
                    live_ref[r] = jnp.max(state["live"])

        lag = 2
        n = len(units)
        for t in range(n + 2 * lag):
            if t < n:
                scores(units[t])
            if 0 <= t - lag < n:
                log_terms(units[t - lag])
            if 0 <= t - 2 * lag < n:
                weights_and_values(units[t - 2 * lag])

    def first_blocks(r, n_blocks):
        g0 = (i * n_sub + r) * tile
        return (r, [g0 - b * tile for b in range(n_blocks)])

    @pl.when(i == 0)
    def _():
        window([first_blocks(r, min(r + 1, 3)) for r in range(n_sub)], True)

    @pl.when(i > 0)
    def _():
        window([first_blocks(r, 3) for r in range(n_sub)], True)

    for r in range(n_sub):
        def cond(j, r=r):
            return jnp.logical_and(j >= 0, live_ref[r] > SB_DEAD_LOG2_WEIGHT)

        def body(j, r=r):
            window([(r, [j * tile])], False)
            return j - 1

        lax.while_loop(cond, body, i * n_sub + r - 3)

    outs = [[acc_ref[r, p] for p in pairs] for r in range(n_sub)]
    sumsq = [[_dot(jnp.concatenate(_split_bf16(o * o), axis=1), head_ones_ref[...]) for o in row] for row in outs]
    for r in range(n_sub):
        for p in pairs:
            inv = lax.rsqrt(sumsq[r][p] * (1.0 / SB_HEAD_DIM) + EPS)
            o_ref[r * tile:(r + 1) * tile, p * LANES:(p + 1) * LANES] = (
                outs[r][p] * inv * g_ref[:, p * LANES:(p + 1) * LANES]).astype(BF16)


def _sb_attention(p_act, g, *, batch, seq, tile, n_sub):
    T = batch * seq
    tq = n_sub * tile
    nq = seq // tq
    n_pairs = SB_HEADS // 2
    later2, not_causal, head_ones = _sb_constants(tile)
    return pl.pallas_call(
        functools.partial(_sb_kernel, tile=tile, n_sub=n_sub),
        grid=(batch, nq),
        in_specs=[
            pl.BlockSpec((tq, SB_WIDTH), lambda b, i: (b * nq + i, P_SB_Q // SB_WIDTH)),
            pl.BlockSpec((seq, SB_WIDTH), lambda b, i: (b, P_SB_K // SB_WIDTH)),
            pl.BlockSpec((seq, SB_WIDTH), lambda b, i: (b, P_SB_V // SB_WIDTH)),
            pl.BlockSpec((1, SB_WIDTH), lambda b, i: (0, 0)),
            pl.BlockSpec(later2.shape, lambda b, i: (0, 0)),
            pl.BlockSpec(not_causal.shape, lambda b, i: (0, 0)),
            pl.BlockSpec(head_ones.shape, lambda b, i: (0, 0)),
        ],
        out_specs=pl.BlockSpec((tq, SB_WIDTH), lambda b, i: (b * nq + i, 0)),
        out_shape=jax.ShapeDtypeStruct((T, SB_WIDTH), BF16),
        scratch_shapes=[
            pltpu.VMEM((2, seq, SB_WIDTH), BF16),
            pltpu.VMEM((2, seq, SB_WIDTH), BF16),
            pltpu.VMEM((n_sub, n_pairs, tile, LANES), F32),
            pltpu.VMEM((n_sub, n_pairs, tile, 2 * tile), F32),
            pltpu.SMEM((n_sub,), F32),
        ],
        compiler_params=pltpu.CompilerParams(
            dimension_semantics=("arbitrary", "arbitrary"), vmem_limit_bytes=VMEM_LIMIT_BYTES),
        name="sb",
    )(p_act, p_act, p_act, g, later2, not_causal, head_ones)


def _gla_kernel(q_ref, k_ref, v_ref, og_ref, glr_ref, wg_ref, bg_ref, g_ref, o_ref, state_ref,
                *, seq, step_rows):
    C = GLA_CHUNK
    R = step_rows
    n_chunks = R // C
    pairs = range(GLA_HEADS // 2)
    low_half = lax.broadcasted_iota(jnp.int32, (1, LANES), 1) < GLA_DK
    ri = lax.broadcasted_iota(jnp.int32, (R, R), 0)
    ci = lax.broadcasted_iota(jnp.int32, (R, R), 1)
    same_chunk = (ri // C) == (ci // C)
    causal = same_chunk & (ci <= ri)
    prefix = jnp.where(causal, 1.0, 0.0).astype(BF16)
    prefix2 = jnp.concatenate([prefix, prefix], axis=1)
    state_ref[...] = jnp.zeros_like(state_ref)

    def front(s, out):
        rows = pl.ds(pl.multiple_of(s * R, R), R)
        log_a = _log_sigmoid(_dot(glr_ref[rows, :], wg_ref[...]) + bg_ref[...]) * (1.0 / GLA_GATE_NORMALIZER)
        la_hi, la_lo = _split_bf16(log_a)
        b = _dot(prefix2, jnp.concatenate([la_hi, la_lo], axis=0))
        yield
        b_last = [b[c * C + C - 1:(c + 1) * C, :] for c in range(n_chunks)]
        b_end = jnp.concatenate([jnp.broadcast_to(r, (C, r.shape[1])) for r in b_last], axis=0)
        q = q_ref[rows, :].astype(F32) * (GLA_DK ** -0.5)
        k = k_ref[rows, :].astype(F32)
        q_dec = q * jnp.exp(b)
        k_inv = (k * jnp.exp(-b)).astype(BF16)
        yield
        k_end = k * jnp.exp(b_end - b)
        decay = [jnp.exp(r) for r in b_last]
        q_half, k_end_half = [], []
        for p in pairs:
            ks = slice(p * LANES, (p + 1) * LANES)
            q_half.append((jnp.where(low_half, q_dec[:, ks], 0.0).astype(BF16),
                           jnp.where(low_half, 0.0, q_dec[:, ks]).astype(BF16)))
            k_end_half.append((jnp.where(low_half, k_end[:, ks], 0.0).astype(BF16),
                               jnp.where(low_half, 0.0, k_end[:, ks]).astype(BF16)))
        yield
        v = [v_ref[rows, h * GLA_DV:(h + 1) * GLA_DV] for h in range(GLA_HEADS)]
        attn, o_intra = [], []
        for h in range(GLA_HEADS):
            scores = _dot_nt(q_half[h // 2][h % 2], k_inv[:, (h // 2) * LANES:(h // 2 + 1) * LANES])
            attn.append(jnp.where(causal, scores, 0.0).astype(BF16))
            if h % 2 == 1:
                yield
        for h in range(GLA_HEADS):
            o_intra.append(_dot(attn[h], v[h]))
        yield
        kv = []
        for p in pairs:
            kv.append([_dot_tn(jnp.concatenate([v[2 * p][c * C:(c + 1) * C], v[2 * p + 1][c * C:(c + 1) * C]], axis=0),
                               jnp.concatenate([k_end_half[p][0][c * C:(c + 1) * C],
                                                k_end_half[p][1][c * C:(c + 1) * C]], axis=0))
                       for c in range(n_chunks)])
            yield
        out.extend([rows, q_half, decay, o_intra, kv])

    def back(rows, q_half, decay, o_intra, kv):
        o_inter = [[] for _ in range(GLA_HEADS)]
        for p in pairs:
            state_t = state_ref[p]
            for c in range(n_chunks):
                state_b = state_t.astype(BF16)
                for hh in range(2):
                    o_inter[2 * p + hh].append(_dot_nt(q_half[p][hh][c * C:(c + 1) * C], state_b))
                state_t = state_t * decay[c][:, p * LANES:(p + 1) * LANES] + kv[p][c]
            state_ref[p] = state_t
            yield
        for h in range(GLA_HEADS):
            vs = slice(h * GLA_DV, (h + 1) * GLA_DV)
            o = o_intra[h] + jnp.concatenate(o_inter[h], axis=0)
            inv = lax.rsqrt(jnp.mean(o * o, axis=-1, keepdims=True) + EPS)
            gate = og_ref[rows, vs].astype(F32)
            gate = gate / (1.0 + jnp.exp(-gate))
            o_ref[rows, vs] = (o * inv * g_ref[:, vs] * gate).astype(BF16)
            yield

    def interleave(*gens):
        live = list(gens)
        while live:
            for gen in list(live):
                try:
                    next(gen)
                except StopIteration:
                    live.remove(gen)

    def group(gi, carry):
        s0 = gi * steps_per_group
        pending = []
        interleave(front(s0, pending))
        for d in range(1, steps_per_group):
            ahead = []
            interleave(front(s0 + d, ahead), back(*pending))
            pending = ahead
        interleave(back(*pending))
        return carry

    n_steps = seq // R
    steps_per_group = 8 if n_steps % 8 == 0 else 1
    lax.fori_loop(0, n_steps // steps_per_group, group, 0)


def _gla(p_act, g_lr, w_gate, b_gate, g, *, batch, seq):
    T = batch * seq
    n_pairs = GLA_HEADS // 2
    return pl.pallas_call(
        functools.partial(_gla_kernel, seq=seq, step_rows=min(256, seq)),
        grid=(batch,),
        in_specs=[
            pl.BlockSpec((seq, GLA_KEY_WIDTH), lambda b: (b, P_G_Q // GLA_KEY_WIDTH)),
            pl.BlockSpec((seq, GLA_KEY_WIDTH), lambda b: (b, P_G_K // GLA_KEY_WIDTH)),
            pl.BlockSpec((seq, GLA_WIDTH), lambda b: (b, P_G_V // GLA_WIDTH)),
            pl.BlockSpec((seq, GLA_WIDTH), lambda b: (b, P_G_OG // GLA_WIDTH)),
            pl.BlockSpec((seq, LANES), lambda b: (b, 0)),
            pl.BlockSpec((LANES, GLA_KEY_WIDTH), lambda b: (0, 0)),
            pl.BlockSpec((1, GLA_KEY_WIDTH), lambda b: (0, 0)),
            pl.BlockSpec((1, GLA_WIDTH), lambda b: (0, 0)),
        ],
        out_specs=pl.BlockSpec((seq, GLA_WIDTH), lambda b: (b, 0)),
        out_shape=jax.ShapeDtypeStruct((T, GLA_WIDTH), BF16),
        scratch_shapes=[pltpu.VMEM((n_pairs, GLA_DV, LANES), F32)],
        compiler_params=pltpu.CompilerParams(
            dimension_semantics=("arbitrary",), vmem_limit_bytes=VMEM_LIMIT_BYTES),
        name="gla",
    )(p_act, p_act, p_act, p_act, g_lr, w_gate, b_gate, g)


def _rms(x, g):
    ms = jnp.mean(x * x, axis=-1, keepdims=True)
    return x * lax.rsqrt(ms + EPS) * g


def _ffn_kernel(x_ref, osb_ref, ogla_ref, wo_sb_ref, wo_gla_ref, g2_ref, wup_ref, cw_ref, cb_ref,
                wdn_ref, gf_ref, out_ref, x1_ref, h2_ref, acc_ref, u00_ref, u01_ref, u10_ref, u11_ref,
                halo_ref,
                *, tm, fc, tiles_per_step, steps_per_seq):
    t = pl.program_id(0)
    n_chunks = D_FF // fc
    H = SUBLANES
    n_pieces = 4
    u_bufs = ((u00_ref, u01_ref), (u10_ref, u11_ref))

    @pl.when(t % steps_per_seq == 0)
    def _():
        halo_ref[...] = jnp.zeros_like(halo_ref)

    def cols_of(part, c):
        return slice(part * D_FF + c * fc, part * D_FF + (c + 1) * fc)

    def start(j):
        rows = slice(j * tm, (j + 1) * tm)
        piece = D_MODEL // n_pieces
        for k in range(n_pieces):
            cols = slice(k * piece, (k + 1) * piece)
            x1_ref[j, :, cols] = (x_ref[rows, cols] + _dot(osb_ref[rows, :], wo_sb_ref[:, cols])
                                  + _dot(ogla_ref[rows, :], wo_gla_ref[:, cols]))
            yield
        h2_ref[...] = _rms(x1_ref[j], g2_ref[...]).astype(BF16)
        yield

    def finish(j):
        piece = tm // n_pieces
        for k in range(n_pieces):
            rows = slice(k * piece, (k + 1) * piece)
            out_ref[j * tm + k * piece:j * tm + (k + 1) * piece, :] = _rms(
                x1_ref[j, rows, :] + acc_ref[j, rows, :], gf_ref[...])
            yield

    def up_project(c, part):
        buf = u_bufs[c % 2][part]
        u = _dot(h2_ref[...], wup_ref[:, cols_of(part, c)])
        buf[0:H, :] = halo_ref[part, c]
        buf[H:H + tm, :] = u
        halo_ref[part, c] = u[tm - H:tm, :]

    def conv_gate_down(j, c, r0, nrows):
        conv = []
        for part in range(2):
            buf = u_bufs[c % 2][part]
            cw = cw_ref[:, cols_of(part, c)]
            win = buf[r0:H + r0 + nrows, :]
            conv.append(cb_ref[:, cols_of(part, c)] + cw[2:3, :] * win[H:, :]
                        + cw[1:2, :] * pltpu.roll(win, 1, axis=0)[H:, :]
                        + cw[0:1, :] * pltpu.roll(win, 2, axis=0)[H:, :])
        a, val = conv
        half_a = 0.5 * a
        gated = ((half_a + half_a * jnp.tanh(half_a)) * val).astype(BF16)
        down = _dot(gated, wdn_ref[c * fc:(c + 1) * fc, :])
        if c == 0:
            acc_ref[j, r0:r0 + nrows, :] = down
        else:
            acc_ref[j, r0:r0 + nrows, :] += down

    def conv_ffn(j):
        half = tm // 2
        up_project(0, 0)
        up_project(0, 1)
        for c in range(n_chunks):
            for part in range(2):
                if c + 1 < n_chunks:
                    up_project(c + 1, part)
                conv_gate_down(j, c, part * half, half)

    def interleave(*gens):
        live = list(gens)
        while live:
            for gen in list(live):
                try:
                    next(gen)
                except StopIteration:
                    live.remove(gen)

    interleave(start(0))
    for j in range(tiles_per_step):
        conv_ffn(j)
        if j + 1 < tiles_per_step:
            interleave(start(j + 1), finish(j))
        else:
            interleave(finish(j))


def _ffn(x2d, o_sb, o_gla, wo_sb, wo_gla, g2, w_up, conv_w, conv_b, w_down, gf, *, tm, fc, seq):
    T = x2d.shape[0]
    n_chunks = D_FF // fc
    tiles_per_step = 2 if seq % (2 * tm) == 0 else 1
    rows = tiles_per_step * tm
    const = lambda i: (0, 0)
    resident = functools.partial(pl.BlockSpec, index_map=const, pipeline_mode=pl.Buffered(1))
    return pl.pallas_call(
        functools.partial(_ffn_kernel, tm=tm, fc=fc, tiles_per_step=tiles_per_step,
                          steps_per_seq=seq // rows),
        grid=(T // rows,),
        in_specs=[
            pl.BlockSpec((rows, D_MODEL), lambda i: (i, 0)),
            pl.BlockSpec((rows, SB_WIDTH), lambda i: (i, 0)),
            pl.BlockSpec((rows, GLA_WIDTH), lambda i: (i, 0)),
            resident((SB_WIDTH, D_MODEL)),
            resident((GLA_WIDTH, D_MODEL)),
            resident((1, D_MODEL)),
            resident((D_MODEL, 2 * D_FF)),
            resident((CONV_WIDTH, 2 * D_FF)),
            resident((1, 2 * D_FF)),
            resident((D_FF, D_MODEL)),
            resident((1, D_MODEL)),
        ],
        out_specs=pl.BlockSpec((rows, D_MODEL), lambda i: (i, 0)),
        out_shape=jax.ShapeDtypeStruct((T, D_MODEL), F32),
        scratch_shapes=[
            pltpu.VMEM((tiles_per_step, tm, D_MODEL), F32),
            pltpu.VMEM((tm, D_MODEL), BF16),
            pltpu.VMEM((tiles_per_step, tm, D_MODEL), F32),
            pltpu.VMEM((SUBLANES + tm, fc), F32),
            pltpu.VMEM((SUBLANES + tm, fc), F32),
            pltpu.VMEM((SUBLANES + tm, fc), F32),
            pltpu.VMEM((SUBLANES + tm, fc), F32),
            pltpu.VMEM((2, n_chunks, SUBLANES, fc), F32),
        ],
        compiler_params=pltpu.CompilerParams(
            dimension_semantics=("arbitrary",), vmem_limit_bytes=VMEM_LIMIT_BYTES),
        name="ffn",
    )(x2d, o_sb, o_gla, wo_sb, wo_gla, g2, w_up, conv_w, conv_b, w_down, gf)


def kernel(x, attn_norm_g, w_in, w_gate_up, b_gate_up, sb_out_g, gla_out_g, w_out, ffn_norm_g,
           w_ffn_up, conv_w, conv_b, w_ffn_down, final_norm_g):
    batch, seq, d_model = x.shape
    assert d_model == D_MODEL and attn_norm_g.shape[0] == 1
    T = batch * seq
    x2d = x.reshape(T, D_MODEL)

    lr0 = P_G_OG
    w_in0 = w_in[0]
    sb_q_scale = -(SB_HEAD_DIM ** -0.5) * LOG2_E
    w_main = jnp.concatenate([w_in0[:, :SB_WIDTH] * sb_q_scale, w_in0[:, SB_WIDTH:lr0],
                              w_in0[:, lr0 + GLA_GATE_RANK:]], axis=1).astype(BF16)
    w_lr = jnp.pad(w_in0[:, lr0:lr0 + GLA_GATE_RANK], ((0, 0), (0, LANES - GLA_GATE_RANK))).astype(BF16)
    w_gate = jnp.pad(w_gate_up[0], ((0, LANES - GLA_GATE_RANK), (0, 0))).astype(BF16)

    tm = min(512, seq)
    p_act, g_lr = _inproj(x2d, attn_norm_g, w_main, w_lr, tm=tm)
    o_sb = _sb_attention(p_act, sb_out_g, batch=batch, seq=seq, tile=LANES, n_sub=8)
    o_gla = _gla(p_act, g_lr, w_gate, b_gate_up, gla_out_g, batch=batch, seq=seq)
    out = _ffn(x2d, o_sb, o_gla,
               w_out[0, :SB_WIDTH].astype(BF16), w_out[0, SB_WIDTH:].astype(BF16),
               ffn_norm_g, w_ffn_up[0].astype(BF16), conv_w[0], conv_b, w_ffn_down[0].astype(BF16),
               final_norm_g.reshape(1, D_MODEL), tm=tm, fc=256, seq=seq)
    return out.reshape(batch, seq, D_MODEL)
```
